```python
import functools
import jax, jax.numpy as jnp
from jax import lax
import numpy as np

D_MODEL = 2048
BATCH = 4
SEQ = 2048
DEPTH = 4
DEC_BATCH = 128
DEC_SEQ = 8
PAST_LEN = 16384
PAGE_SIZE = 128

MIX_W = 1024
CHUNK = 128
SGU_HEADS = 8
SGU_HD = MIX_W // SGU_HEADS
CONV_W = 3
POOL_WINDOWS = (2, 4, 8, 16)
POOL_GROUPS = 4
POOL_GD = MIX_W // POOL_GROUPS
POOL_BUF = 15
N_BRANCH = 3
MIX_IN_COLS = 6 * MIX_W + N_BRANCH * D_MODEL
D_FF = 5632
N_EXPERTS = 8
TOP_K = 2
E_FF = 7168
MOE_BLOCK = 128
N_DENSE = (DEPTH + 1) // 2
N_MOE = DEPTH // 2
DEEPNORM_ALPHA = (2 * DEPTH) ** 0.25
DEEPNORM_BETA = (8 * DEPTH) ** -0.25
LN_EPS = 1e-5

kernel_name = 'hybrid_gated_branch_decoder_step'


def layer_norm(x, g, b):
    xf = x.astype(jnp.float32)
    mu = xf.mean(-1, keepdims=True)
    var = jnp.square(xf - mu).mean(-1, keepdims=True)
    return ((xf - mu) * lax.rsqrt(var + LN_EPS)).astype(x.dtype) * g + b


def modulate(x, shift, scale):
    return x * (1 + scale[:, None, :]) + shift[:, None, :]


def sgu_chunk_mix(v, w_s, b_s):
    bsz, L, _ = v.shape
    n = -(-L // CHUNK)
    vp = jnp.pad(v, ((0, 0), (0, n * CHUNK - L), (0, 0))).reshape(bsz, n, CHUNK, SGU_HEADS, SGU_HD)
    mask = jnp.tril(jnp.ones((CHUNK, CHUNK), dtype=bool))
    w = jnp.where(mask[None], w_s, jnp.zeros_like(w_s)).astype(v.dtype)
    s = jnp.einsum('hts,bnshd->bnthd', w, vp) + b_s.T[None, None, :, :, None]
    return s.reshape(bsz, n * CHUNK, MIX_W)[:, :L]


def short_conv(z, buf, conv_w, conv_b):
    L = z.shape[1]
    zc = jnp.concatenate([buf, z], axis=1)
    y = conv_b + zc[:, 0:L] * conv_w[0]
    for k in range(1, CONV_W):
        y = y + zc[:, k:k + L] * conv_w[k]
    return y, zc[:, -(CONV_W - 1):]


def multiscale_pool(p, buf, pos, w_pool, pool_scale):
    L = p.shape[1]
    pc = jnp.concatenate([buf, p], axis=1)
    cs = jnp.cumsum(pc.astype(jnp.float32), axis=1)
    cs = jnp.pad(cs, ((0, 0), (1, 0), (0, 0)))
    end = cs[:, POOL_BUF + 1:POOL_BUF + 1 + L]
    means = []
    for j, w in enumerate(POOL_WINDOWS):
        sl = slice(j * POOL_GD, (j + 1) * POOL_GD)
        start = cs[:, POOL_BUF + 1 - w:POOL_BUF + 1 - w + L, sl]
        cnt = jnp.minimum(w, pos + 1).astype(jnp.float32)[None, :, None]
        means.append((end[..., sl] - start) / cnt)
    pooled = jnp.concatenate(means, axis=-1).astype(p.dtype) - p
    bsz = p.shape[0]
    mixed = jnp.einsum('blgi,gio->blgo', pooled.reshape(bsz, L, POOL_GROUPS, POOL_GD), w_pool).reshape(bsz, L, MIX_W)
    return mixed * pool_scale, pc[:, -POOL_BUF:]


def swiglu(h, wg, wu, wd):
    return (jax.nn.silu(h @ wg) * (h @ wu)) @ wd


def moe_ffn(h, w_router, w_gate, w_up, w_down):
    bsz, L, _ = h.shape
    n_tok = bsz * L
    n_assign = n_tok * TOP_K
    xt = h.reshape(n_tok, D_MODEL)
    probs = jax.nn.softmax((xt @ w_router).astype(jnp.float32), axis=-1)
    top_p, top_e = lax.top_k(probs, TOP_K)
    top_p = top_p / top_p.sum(-1, keepdims=True)
    flat_e = top_e.reshape(-1)
    flat_t = jnp.repeat(jnp.arange(n_tok, dtype=jnp.int32), TOP_K)
    flat_w = top_p.reshape(-1)
    order = jnp.argsort(flat_e)
    e_s, t_s, w_s = flat_e[order], flat_t[order], flat_w[order]
    counts = jnp.zeros((N_EXPERTS,), jnp.int32).at[flat_e].add(1)
    start = jnp.cumsum(counts) - counts
    padded = (counts + MOE_BLOCK - 1) // MOE_BLOCK * MOE_BLOCK
    pad_end = jnp.cumsum(padded)
    pad_start = pad_end - padded
    dest = pad_start[e_s] + jnp.arange(n_assign, dtype=jnp.int32) - start[e_s]
    n_blocks = -(-n_assign // MOE_BLOCK) + N_EXPERTS
    rows = jnp.zeros((n_blocks * MOE_BLOCK, D_MODEL), h.dtype).at[dest].set(xt[t_s])
    block_e = jnp.minimum(jnp.searchsorted(pad_end, jnp.arange(n_blocks, dtype=jnp.int32) * MOE_BLOCK, side='right'), N_EXPERTS - 1)

    def expert_block(args):
        xb, e = args
        return swiglu(xb, w_gate[e], w_up[e], w_down[e])

    y_rows = lax.map(expert_block, (rows.reshape(n_blocks, MOE_BLOCK, D_MODEL), block_e)).reshape(-1, D_MODEL)
    out = jnp.zeros_like(xt).at[t_s].add(y_rows[dest] * w_s[:, None].astype(h.dtype))
    return out.reshape(h.shape)


def token_mixer(h, conv_buf, pool_buf, pos, lw):
    proj = h @ lw['w_mix_in']
    u, v, gate_b, gate_c, h_b, p, g = jnp.split(proj, [MIX_W * i for i in range(1, 7)], axis=-1)
    v = layer_norm(v, lw['sgu_g'], lw['sgu_b'])
    y_a = u * sgu_chunk_mix(v, lw['w_spatial'], lw['b_spatial'])
    conv_out, new_conv = short_conv(gate_c * h_b, conv_buf, lw['conv_w'], lw['conv_b'])
    y_b = gate_b * conv_out
    y_c, new_pool = multiscale_pool(p, pool_buf, pos, lw['w_pool'], lw['pool_scale'])
    branches = jnp.stack([y_a, y_b, y_c], axis=-2)
    br = jnp.einsum('blri,rid->blrd', branches, lw['w_branch'])
    gates = jax.nn.sigmoid(g.reshape(g.shape[0], g.shape[1], N_BRANCH, D_MODEL))
    merged = (gates * br).sum(-2)
    return merged @ lw['w_o'], v, new_conv, new_pool


def decoder_layer(x, c, conv_buf, pool_buf, pos, lw, ffn):
    ada = jnp.einsum('bd,ndk->bnk', jax.nn.silu(c), lw['w_ada']) + lw['b_ada']
    sh1, sc1, g1 = jnp.split(ada[:, 0], 3, axis=-1)
    sh2, sc2, g2 = jnp.split(ada[:, 1], 3, axis=-1)
    mix, v, new_conv, new_pool = token_mixer(modulate(x, sh1, sc1), conv_buf, pool_buf, pos, lw)
    x = layer_norm(DEEPNORM_ALPHA * x + g1[:, None, :] * mix, lw['ln_g'][0], lw['ln_b'][0])
    f = ffn(modulate(x, sh2, sc2))
    x = layer_norm(DEEPNORM_ALPHA * x + g2[:, None, :] * f, lw['ln_g'][1], lw['ln_b'][1])
    return x, v, new_conv, new_pool


def setup_inputs(seed: int = 0) -> dict:
    key = jax.random.key(seed)
    ks = jax.random.split(key, 30)

    def nrm(k, shape, scale=1.0):
        return scale * jax.random.normal(k, shape, jnp.float32)

    D = D_MODEL
    return {
        'x_prompt': nrm(ks[0], (BATCH, SEQ, D)),
        'x_sample': nrm(ks[1], (DEC_BATCH, DEC_SEQ, D)),
        'state_conv': nrm(ks[2], (DEPTH, DEC_BATCH, CONV_W - 1, MIX_W)),
        'state_pool': nrm(ks[3], (DEPTH, DEC_BATCH, POOL_BUF, MIX_W)),
        'c_prompt': nrm(ks[4], (BATCH, D)),
        'c_sample': nrm(ks[5], (DEC_BATCH, D)),
        'ln_in_g': 1.0 + nrm(ks[6], (D,), 0.1),
        'ln_in_b': nrm(ks[7], (D,), 0.1),
        'w_ada': nrm(ks[8], (DEPTH, 2, D, 3 * D), 0.5 * D ** -0.5),
        'b_ada': nrm(ks[9], (DEPTH, 2, 3 * D), 0.1),
        'w_mix_in': nrm(ks[10], (DEPTH, D, MIX_IN_COLS), D ** -0.5),
        'sgu_g': 1.0 + nrm(ks[11], (DEPTH, MIX_W), 0.1),
        'sgu_b': nrm(ks[12], (DEPTH, MIX_W), 0.1),
        'w_spatial': nrm(ks[13], (DEPTH, SGU_HEADS, CHUNK, CHUNK), CHUNK ** -0.5),
        'b_spatial': 1.0 + nrm(ks[14], (DEPTH, SGU_HEADS, CHUNK), 0.1),
        'conv_w': nrm(ks[15], (DEPTH, CONV_W, MIX_W), CONV_W ** -0.5),
        'conv_b': nrm(ks[16], (DEPTH, MIX_W), 0.1),
        'w_pool': nrm(ks[17], (DEPTH, POOL_GROUPS, POOL_GD, POOL_GD), POOL_GD ** -0.5),
        'pool_scale': 1.0 + nrm(ks[18], (DEPTH, MIX_W), 0.1),
        'w_branch': nrm(ks[19], (DEPTH, N_BRANCH, MIX_W, D), MIX_W ** -0.5),
        'w_o': nrm(ks[20], (DEPTH, D, D), DEEPNORM_BETA * D ** -0.5),
        'ln_g': 1.0 + nrm(ks[21], (DEPTH, 2, D), 0.1),
        'ln_b': nrm(ks[22], (DEPTH, 2, D), 0.1),
        'w_ffn_gate': nrm(ks[23], (N_DENSE, D, D_FF), D ** -0.5),
        'w_ffn_up': nrm(ks[24], (N_DENSE, D, D_FF), D ** -0.5),
        'w_ffn_down': nrm(ks[25], (N_DENSE, D_FF, D), DEEPNORM_BETA * D_FF ** -0.5),
        'w_router': nrm(ks[26], (N_MOE, D, N_EXPERTS), D ** -0.5),
        'w_exp_gate': nrm(ks[27], (N_MOE, N_EXPERTS, D, E_FF), D ** -0.5),
        'w_exp_up': nrm(ks[28], (N_MOE, N_EXPERTS, D, E_FF), D ** -0.5),
        'w_exp_down': nrm(ks[29], (N_MOE, N_EXPERTS, E_FF, D), DEEPNORM_BETA * E_FF ** -0.5),
    }


def reference(x_prompt, x_sample, state_conv, state_pool, c_prompt, c_sample, ln_in_g, ln_in_b,
              w_ada, b_ada, w_mix_in, sgu_g, sgu_b, w_spatial, b_spatial, conv_w, conv_b,
              w_pool, pool_scale, w_branch, w_o, ln_g, ln_b, w_ffn_gate, w_ffn_up, w_ffn_down,
              w_router, w_exp_gate, w_exp_up, w_exp_down):
    bsz_p, seq_p, _ = x_prompt.shape
    bsz_s, seq_s, _ = x_sample.shape
    pos_p = jnp.arange(seq_p, dtype=jnp.int32)
    pos_s = PAST_LEN + jnp.arange(seq_s, dtype=jnp.int32)
    xp = layer_norm(x_prompt, ln_in_g, ln_in_b)
    xs = layer_norm(x_sample, ln_in_g, ln_in_b)
    conv0 = jnp.zeros((bsz_p, CONV_W - 1, MIX_W), x_prompt.dtype)
    pool0 = jnp.zeros((bsz_p, POOL_BUF, MIX_W), x_prompt.dtype)
    conv_p_l, conv_s_l, pool_p_l, pool_s_l, v_s_l = [], [], [], [], []
    for l in range(DEPTH):
        lw = {
            'w_ada': w_ada[l], 'b_ada': b_ada[l], 'w_mix_in': w_mix_in[l],
            'sgu_g': sgu_g[l], 'sgu_b': sgu_b[l], 'w_spatial': w_spatial[l], 'b_spatial': b_spatial[l],
            'conv_w': conv_w[l], 'conv_b': conv_b[l], 'w_pool': w_pool[l], 'pool_scale': pool_scale[l],
            'w_branch': w_branch[l], 'w_o': w_o[l], 'ln_g': ln_g[l], 'ln_b': ln_b[l],
        }
        i = l // 2
        if l % 2 == 0:
            ffn = functools.partial(swiglu, wg=w_ffn_gate[i], wu=w_ffn_up[i], wd=w_ffn_down[i])
        else:
            ffn = functools.partial(moe_ffn, w_router=w_router[i], w_gate=w_exp_gate[i],
                                    w_up=w_exp_up[i], w_down=w_exp_down[i])
        xp, _, conv_p, pool_p = decoder_layer(xp, c_prompt, conv0, pool0, pos_p, lw, ffn)
        xs, v_s, conv_s, pool_s = decoder_layer(xs, c_sample, state_conv[l], state_pool[l], pos_s, lw, ffn)
        conv_p_l.append(conv_p)
        conv_s_l.append(conv_s)
        pool_p_l.append(pool_p)
        pool_s_l.append(pool_s)
        v_s_l.append(v_s)
    new_conv_prompt = jnp.stack(conv_p_l, 0)
    new_conv_sample = jnp.stack(conv_s_l, 0)
    new_pool_prompt = jnp.stack(pool_p_l, 0)
    new_pool_sample = jnp.stack(pool_s_l, 0)
    new_sgu_v_sample = jnp.stack(v_s_l, 0)
    return (xp, xs, new_conv_prompt, new_conv_sample, new_pool_prompt, new_pool_sample, new_sgu_v_sample)
```

```python
import functools

import jax
import jax.numpy as jnp
from jax import lax
from jax.experimental import pallas as pl
from jax.experimental.pallas import tpu as pltpu

D_MODEL = 2048
BATCH = 4
SEQ = 2048
DEPTH = 4
DEC_BATCH = 128
DEC_SEQ = 8
PAST_LEN = 16384
MIX_W = 1024
CHUNK = 128
SGU_HEADS = 8
SGU_HD = MIX_W // SGU_HEADS
CONV_W = 3
POOL_WINDOWS = (2, 4, 8, 16)
POOL_GROUPS = 4
POOL_GD = MIX_W // POOL_GROUPS
POOL_BUF = 15
N_BRANCH = 3
MIX_IN_COLS = 6 * MIX_W + N_BRANCH * D_MODEL
D_FF = 5632
N_EXPERTS = 8
TOP_K = 2
E_FF = 7168
DEEPNORM_ALPHA = (2 * DEPTH) ** 0.25
LN_EPS = 1e-5

P_ROWS = BATCH * SEQ
S_ROWS = DEC_SEQ * DEC_BATCH
M_ROWS = P_ROWS + S_ROWS
N_GROUPS = BATCH + 1
SLAB = DEC_BATCH
C_ROWS = 136
ROUTER_LANES = 128
MOE_BLK = 256
N_ASSIGN = M_ROWS * TOP_K
MOE_ROWS = N_ASSIGN + N_EXPERTS * MOE_BLK
MOE_NB = MOE_ROWS // MOE_BLK

BF = jnp.bfloat16
F32 = jnp.float32

_MIB = 1 << 20


def _params(semantics, vmem_mib):
    return pltpu.CompilerParams(dimension_semantics=semantics, vmem_limit_bytes=vmem_mib * _MIB)


def _dot(a, b):
    return jnp.dot(a, b, preferred_element_type=F32)


def _ln(x, g, b):
    mu = jnp.mean(x, axis=-1, keepdims=True)
    xc = x - mu
    var = jnp.mean(xc * xc, axis=-1, keepdims=True)
    return xc * lax.rsqrt(var + LN_EPS) * g + b


def _slab_bcast(fn, x, *slabs):
    rows, d = x.shape
    x3 = x.reshape(rows // SLAB, SLAB, d)
    return fn(x3, *[s[None] for s in slabs]).reshape(rows, d)


def _modulate(x, shift, scale):
    return _slab_bcast(lambda a, sh, sc: a * (1.0 + sc) + sh, x, shift, scale)


def _group_of_tile(i, tm):
    return jnp.minimum(i * tm // SEQ, BATCH)


def _mod_spec(l, n, k, tm, grid_rank, row_axis):
    def index(*ids):
        return (l, n, k, _group_of_tile(ids[row_axis], tm), 0, 0)

    del grid_rank
    return pl.BlockSpec((None, None, None, None, SLAB, D_MODEL), index)


def _ada_kernel(c_ref, w_ref, b_ref, o_ref):
    c = c_ref[...]
    s = (c * jax.nn.sigmoid(c)).astype(BF)
    o_ref[...] = _dot(s, w_ref[...].astype(BF)) + b_ref[...]


def _ada_call(c_all, w_ada, b_ada):
    tn = 1024
    n3 = 3 * D_MODEL
    return pl.pallas_call(
        _ada_kernel,
        grid=(DEPTH, 2, n3 // tn),
        in_specs=[
            pl.BlockSpec((C_ROWS, D_MODEL), lambda l, n, j: (0, 0)),
            pl.BlockSpec((None, None, D_MODEL, tn), lambda l, n, j: (l, n, 0, j)),
            pl.BlockSpec((None, None, 1, tn), lambda l, n, j: (l, n, 0, j)),
        ],
        out_specs=pl.BlockSpec((None, None, C_ROWS, tn), lambda l, n, j: (l, n, 0, j)),
        out_shape=jax.ShapeDtypeStruct((DEPTH, 2, C_ROWS, n3), F32),
        compiler_params=_params(("arbitrary",) * 3, 40),
        name="ada",
    )(c_all, w_ada, b_ada.reshape(DEPTH, 2, 1, n3))


def _ln_in_kernel(x_ref, g_ref, b_ref, sh_ref, sc_ref, xo_ref, ho_ref):
    xn = _ln(x_ref[...], g_ref[...], b_ref[...])
    xo_ref[...] = xn
    ho_ref[...] = _modulate(xn, sh_ref[...], sc_ref[...]).astype(BF)


def _ln_in_call(x_all, g, b, mods):
    tm = 512
    row = pl.BlockSpec((tm, D_MODEL), lambda i: (i, 0))
    vec = pl.BlockSpec((1, D_MODEL), lambda i: (0, 0))
    return pl.pallas_call(
        _ln_in_kernel,
        grid=(M_ROWS // tm,),
        in_specs=[row, vec, vec, _mod_spec(0, 0, 0, tm, 1, 0), _mod_spec(0, 0, 1, tm, 1, 0)],
        out_specs=[row, row],
        out_shape=[jax.ShapeDtypeStruct((M_ROWS, D_MODEL), F32), jax.ShapeDtypeStruct((M_ROWS, D_MODEL), BF)],
        compiler_params=_params(("arbitrary",), 40),
        name="ln_in",
    )(x_all, g.reshape(1, D_MODEL), b.reshape(1, D_MODEL), mods, mods)


def _gmm_kernel(be_ref, lhs_ref, *refs, n_w, swiglu):
    w_refs = refs[:n_w]
    o_ref = refs[n_w]
    wb_refs = refs[n_w + 1:]
    b = pl.program_id(1)
    prev = be_ref[jnp.maximum(b - 1, 0)]
    changed = jnp.logical_or(b == 0, be_ref[b] != prev)

    @pl.when(changed)
    def _():
        for w_ref, wb_ref in zip(w_refs, wb_refs):
            wb_ref[...] = w_ref[...].astype(BF)

    lhs = lhs_ref[...]
    if swiglu:
        a = _dot(lhs, wb_refs[0][...])
        u = _dot(lhs, wb_refs[1][...])
        o_ref[...] = (a * jax.nn.sigmoid(a) * u).astype(o_ref.dtype)
    else:
        o_ref[...] = _dot(lhs, wb_refs[0][...]).astype(o_ref.dtype)


def _gmm_call(lhs, weights, w_layer, block_e, *, bm, tn, swiglu, out_dtype, vmem_mib, name):
    rows, k = lhs.shape
    n = weights[0].shape[-1]
    n_w = len(weights)
    nb = rows // bm
    w_spec = pl.BlockSpec((None, None, k, tn), lambda j, b, be: (w_layer, be[b], 0, j))
    grid_spec = pltpu.PrefetchScalarGridSpec(
        num_scalar_prefetch=1,
        grid=(n // tn, nb),
        in_specs=[pl.BlockSpec((bm, k), lambda j, b, be: (b, 0))] + [w_spec] * n_w,
        out_specs=pl.BlockSpec((bm, tn), lambda j, b, be: (b, j)),
        scratch_shapes=[pltpu.VMEM((k, tn), BF)] * n_w,
    )
    return pl.pallas_call(
        functools.partial(_gmm_kernel, n_w=n_w, swiglu=swiglu),
        grid_spec=grid_spec,
        out_shape=jax.ShapeDtypeStruct((rows, n), out_dtype),
        compiler_params=_params(("arbitrary", "arbitrary"), vmem_mib),
        name=name,
    )(block_e, lhs, *weights)


def _mid_prompt_kernel(u_ref, v_ref, gb_ref, gc_ref, hb_ref, p_ref, sg_ref, sb_ref, wsp_ref, bsp_ref,
                       cw_ref, cb_ref, wpool_ref, pscale_ref,
                       br_ref, convo_ref, poolo_ref, zh_ref, ph_ref, *, rows, tiles_per_seq):
    i = pl.program_id(0)
    tile_in_seq = i % tiles_per_seq

    @pl.when(tile_in_seq == 0)
    def _():
        zh_ref[...] = jnp.zeros_like(zh_ref)
        ph_ref[...] = jnp.zeros_like(ph_ref)

    v = _ln(v_ref[...], sg_ref[...], sb_ref[...])
    vb = v.astype(BF)
    r_id = lax.broadcasted_iota(jnp.int32, (CHUNK, CHUNK), 0)
    c_id = lax.broadcasted_iota(jnp.int32, (CHUNK, CHUNK), 1)
    causal = c_id <= r_id
    w_heads = [jnp.where(causal, wsp_ref[h], jnp.zeros((CHUNK, CHUNK), BF)) for h in range(SGU_HEADS)]
    chunks = []
    for c in range(rows // CHUNK):
        heads = [
            _dot(w_heads[h], vb[c * CHUNK:(c + 1) * CHUNK, h * SGU_HD:(h + 1) * SGU_HD])
            for h in range(SGU_HEADS)
        ]
        chunks.append(jnp.concatenate(heads, axis=1) + bsp_ref[...])
    s = jnp.concatenate(chunks, axis=0)
    br_ref[:, 0:MIX_W] = (u_ref[...] * s).astype(BF)

    z = gc_ref[...] * hb_ref[...]
    ze = jnp.concatenate([zh_ref[...], z], axis=0)
    z1 = pltpu.roll(ze, 1, 0)[8:]
    z2 = pltpu.roll(ze, 2, 0)[8:]
    conv = cb_ref[...] + z2 * cw_ref[0:1, :]
    conv = conv + z1 * cw_ref[1:2, :]
    conv = conv + z * cw_ref[2:3, :]
    br_ref[:, MIX_W:2 * MIX_W] = (gb_ref[...] * conv).astype(BF)

    p = p_ref[...]
    pe = jnp.concatenate([ph_ref[...], p], axis=0)
    win = pe + pltpu.roll(pe, 1, 0)
    sums = [win[16:, 0:POOL_GD]]
    shift = 2
    for j in range(1, POOL_GROUPS):
        win = win[:, POOL_GD:]
        win = win + pltpu.roll(win, shift, 0)
        sums.append(win[16:, 0:POOL_GD])
        shift *= 2
    pos1 = (tile_in_seq * rows + 1 + lax.broadcasted_iota(jnp.int32, (rows, 1), 0)).astype(F32)
    mixed = []
    for j, w in enumerate(POOL_WINDOWS):
        mean = sums[j] / jnp.minimum(float(w), pos1)
        pooled = mean - p[:, j * POOL_GD:(j + 1) * POOL_GD]
        mixed.append(_dot(pooled.astype(BF), wpool_ref[j]))
    y_c = jnp.concatenate(mixed, axis=1) * pscale_ref[...]
    br_ref[:, 2 * MIX_W:3 * MIX_W] = y_c.astype(BF)

    z_tail = z[rows - 8:, :]
    p_tail = p[rows - 16:, :]
    zh_ref[...] = z_tail
    ph_ref[...] = p_tail
    convo_ref[...] = z_tail
    poolo_ref[...] = p_tail


def _mid_prompt_call(proj, l, sgu_g, sgu_b, wsp_bf, bsp_full, conv_w, conv_b, wpool_bf, pool_scale):
    rows = 256
    tiles_per_seq = SEQ // rows

    def col(c):
        return pl.BlockSpec((rows, MIX_W), lambda i, c=c: (i, c))

    vec = pl.BlockSpec((None, 1, MIX_W), lambda i: (l, 0, 0))
    in_specs = [col(c) for c in range(6)] + [
        vec, vec,
        pl.BlockSpec((None, SGU_HEADS, CHUNK, CHUNK), lambda i: (l, 0, 0, 0)),
        pl.BlockSpec((None, CHUNK, MIX_W), lambda i: (l, 0, 0)),
        pl.BlockSpec((None, CONV_W, MIX_W), lambda i: (l, 0, 0)),
        vec,
        pl.BlockSpec((None, POOL_GROUPS, POOL_GD, POOL_GD), lambda i: (l, 0, 0, 0)),
        vec,
    ]
    out_specs = [
        pl.BlockSpec((rows, N_BRANCH * MIX_W), lambda i: (i, 0)),
        pl.BlockSpec((None, 8, MIX_W), lambda i: (i // tiles_per_seq, 0, 0)),
        pl.BlockSpec((None, 16, MIX_W), lambda i: (i // tiles_per_seq, 0, 0)),
    ]
    out_shape = [
        jax.ShapeDtypeStruct((P_ROWS, N_BRANCH * MIX_W), BF),
        jax.ShapeDtypeStruct((BATCH, 8, MIX_W), F32),
        jax.ShapeDtypeStruct((BATCH, 16, MIX_W), F32),
    ]
    return pl.pallas_call(
        functools.partial(_mid_prompt_kernel, rows=rows, tiles_per_seq=tiles_per_seq),
        grid=(P_ROWS // rows,),
        in_specs=in_specs,
        out_specs=out_specs,
        out_shape=out_shape,
        scratch_shapes=[pltpu.VMEM((8, MIX_W), F32), pltpu.VMEM((16, MIX_W), F32)],
        compiler_params=_params(("arbitrary",), 48),
        name="mid_prompt",
    )(proj, proj, proj, proj, proj, proj, sgu_g.reshape(DEPTH, 1, MIX_W), sgu_b.reshape(DEPTH, 1, MIX_W),
      wsp_bf, bsp_full, conv_w, conv_b.reshape(DEPTH, 1, MIX_W), wpool_bf, pool_scale.reshape(DEPTH, 1, MIX_W))


def _mid_sample_kernel(u_ref, v_ref, gb_ref, gc_ref, hb_ref, p_ref, cs_ref, ps_ref, sg_ref, sb_ref, wv_ref,
                       bv_ref, cw_ref, cb_ref, wpool_ref, pscale_ref,
                       br_ref, vo_ref, convo_ref, poolo_ref, *, bs):
    sg = sg_ref[...]
    sb = sb_ref[...]
    v = [_ln(v_ref[t], sg, sb) for t in range(DEC_SEQ)]
    for t in range(DEC_SEQ):
        vo_ref[t] = v[t]
        acc = v[0] * wv_ref[t, 0:1, :]
        for s in range(1, t + 1):
            acc = acc + v[s] * wv_ref[t, s:s + 1, :]
        br_ref[t, :, 0:MIX_W] = (u_ref[t] * (acc + bv_ref[t:t + 1, :])).astype(BF)

    zc = [cs_ref[0], cs_ref[1]] + [gc_ref[t] * hb_ref[t] for t in range(DEC_SEQ)]
    for t in range(DEC_SEQ):
        conv = cb_ref[...] + zc[t] * cw_ref[0:1, :]
        conv = conv + zc[t + 1] * cw_ref[1:2, :]
        conv = conv + zc[t + 2] * cw_ref[2:3, :]
        br_ref[t, :, MIX_W:2 * MIX_W] = (gb_ref[t] * conv).astype(BF)
    convo_ref[0] = zc[DEC_SEQ]
    convo_ref[1] = zc[DEC_SEQ + 1]

    pc = [ps_ref[i] for i in range(POOL_BUF)] + [p_ref[t] for t in range(DEC_SEQ)]
    for i in range(POOL_BUF):
        poolo_ref[i] = pc[DEC_SEQ + i]
    for j, w in enumerate(POOL_WINDOWS):
        lanes = slice(j * POOL_GD, (j + 1) * POOL_GD)
        pooled = []
        for t in range(DEC_SEQ):
            end = POOL_BUF + t
            tot = pc[end][:, lanes]
            for k in range(1, w):
                tot = tot + pc[end - k][:, lanes]
            cnt = float(min(w, PAST_LEN + t + 1))
            pooled.append(tot / cnt - pc[end][:, lanes])
        pooled = jnp.concatenate(pooled, axis=0).astype(BF)
        mixed = _dot(pooled, wpool_ref[j]) * pscale_ref[:, lanes]
        for t in range(DEC_SEQ):
            br_ref[t, :, 2 * MIX_W + j * POOL_GD:2 * MIX_W + (j + 1) * POOL_GD] = (
                mixed[t * bs:(t + 1) * bs].astype(BF))


def _mid_sample_call(proj3, l, state_conv_t, state_pool_t, sgu_g, sgu_b, wv, bv, conv_w, conv_b, wpool_bf,
                     pool_scale):
    bs = 32
    t_block = P_ROWS // SLAB // DEC_SEQ

    def col(c):
        return pl.BlockSpec((DEC_SEQ, bs, MIX_W), lambda i, c=c: (t_block, i, c))

    vec = pl.BlockSpec((None, 1, MIX_W), lambda i: (l, 0, 0))
    in_specs = [col(c) for c in range(6)] + [
        pl.BlockSpec((None, CONV_W - 1, bs, MIX_W), lambda i: (l, 0, i, 0)),
        pl.BlockSpec((None, POOL_BUF, bs, MIX_W), lambda i: (l, 0, i, 0)),
        vec, vec,
        pl.BlockSpec((None, DEC_SEQ, DEC_SEQ, MIX_W), lambda i: (l, 0, 0, 0)),
        pl.BlockSpec((None, DEC_SEQ, MIX_W), lambda i: (l, 0, 0)),
        pl.BlockSpec((None, CONV_W, MIX_W), lambda i: (l, 0, 0)),
        vec,
        pl.BlockSpec((None, POOL_GROUPS, POOL_GD, POOL_GD), lambda i: (l, 0, 0, 0)),
        vec,
    ]
    out_specs = [
        pl.BlockSpec((DEC_SEQ, bs, N_BRANCH * MIX_W), lambda i: (0, i, 0)),
        pl.BlockSpec((DEC_SEQ, bs, MIX_W), lambda i: (0, i, 0)),
        pl.BlockSpec((CONV_W - 1, bs, MIX_W), lambda i: (0, i, 0)),
        pl.BlockSpec((POOL_BUF, bs, MIX_W), lambda i: (0, i, 0)),
    ]
    out_shape = [
        jax.ShapeDtypeStruct((DEC_SEQ, DEC_BATCH, N_BRANCH * MIX_W), BF),
        jax.ShapeDtypeStruct((DEC_SEQ, DEC_BATCH, MIX_W), F32),
        jax.ShapeDtypeStruct((CONV_W - 1, DEC_BATCH, MIX_W), F32),
        jax.ShapeDtypeStruct((POOL_BUF, DEC_BATCH, MIX_W), F32),
    ]
    return pl.pallas_call(
        functools.partial(_mid_sample_kernel, bs=bs),
        grid=(DEC_BATCH // bs,),
        in_specs=in_specs,
        out_specs=out_specs,
        out_shape=out_shape,
        compiler_params=_params(("arbitrary",), 48),
        name="mid_sample",
    )(proj3, proj3, proj3, proj3, proj3, proj3, state_conv_t, state_pool_t,
      sgu_g.reshape(DEPTH, 1, MIX_W), sgu_b.reshape(DEPTH, 1, MIX_W), wv, bv, conv_w,
      conv_b.reshape(DEPTH, 1, MIX_W), wpool_bf, pool_scale.reshape(DEPTH, 1, MIX_W))


def _merge_kernel(br_ref, w_ref, g0_ref, g1_ref, g2_ref, o_ref, wb_ref):
    @pl.when(pl.program_id(1) == 0)
    def _():
        wb_ref[...] = w_ref[...].astype(BF)

    acc = None
    for r, g_ref in enumerate((g0_ref, g1_ref, g2_ref)):
        term = jax.nn.sigmoid(g_ref[...]) * _dot(br_ref[:, r * MIX_W:(r + 1) * MIX_W], wb_ref[r])
        acc = term if acc is None else acc + term
    o_ref[...] = acc.astype(BF)


def _merge_call(br, proj, w_branch, l):
    tm, tn = 512, 512
    gate_col0 = 6 * MIX_W // tn
    per_gate = D_MODEL // tn

    def gate(r):
        return pl.BlockSpec((tm, tn), lambda j, i, r=r: (i, gate_col0 + r * per_gate + j))

    return pl.pallas_call(
        _merge_kernel,
        grid=(D_MODEL // tn, M_ROWS // tm),
        in_specs=[
            pl.BlockSpec((tm, N_BRANCH * MIX_W), lambda j, i: (i, 0)),
            pl.BlockSpec((None, N_BRANCH, MIX_W, tn), lambda j, i: (l, 0, 0, j)),
            gate(0), gate(1), gate(2),
        ],
        out_specs=pl.BlockSpec((tm, tn), lambda j, i: (i, j)),
        out_shape=jax.ShapeDtypeStruct((M_ROWS, D_MODEL), BF),
        scratch_shapes=[pltpu.VMEM((N_BRANCH, MIX_W, tn), BF)],
        compiler_params=_params(("arbitrary", "arbitrary"), 48),
        name="merge",
    )(br, w_branch, proj, proj, proj)


def _residual_ln_epilogue(f, x_ref, gate_ref, lng_ref, lnb_ref, sh_ref, sc_ref, wr_ref,
                          xo_ref, ho_ref, te_ref, tw_ref):
    y = _slab_bcast(lambda a, g: a * g, f, gate_ref[...]) + DEEPNORM_ALPHA * x_ref[...]
    xn = _ln(y, lng_ref[...], lnb_ref[...])
    xo_ref[...] = xn
    if ho_ref is None:
        return
    h = _modulate(xn, sh_ref[...], sc_ref[...])
    ho_ref[...] = h.astype(BF)
    if wr_ref is None:
        return
    rows = h.shape[0]
    logits = jnp.dot(h, wr_ref[...], preferred_element_type=F32, precision=lax.Precision.HIGHEST)
    lane = lax.broadcasted_iota(jnp.int32, (rows, ROUTER_LANES), 1)
    lane_f = lane.astype(F32)
    valid = lane < N_EXPERTS
    lg = jnp.where(valid, logits, -jnp.inf)
    ex = jnp.exp(lg - jnp.max(lg, axis=-1, keepdims=True))
    probs = ex / jnp.sum(ex, axis=-1, keepdims=True)
    cand = jnp.where(valid, probs, -1.0)
    p1 = jnp.max(cand, axis=-1, keepdims=True)
    i1 = jnp.min(jnp.where(cand == p1, lane_f, float(ROUTER_LANES)), axis=-1, keepdims=True)
    cand = jnp.where(lane_f == i1, -1.0, cand)
    p2 = jnp.max(cand, axis=-1, keepdims=True)
    i2 = jnp.min(jnp.where(cand == p2, lane_f, float(ROUTER_LANES)), axis=-1, keepdims=True)
    tot = p1 + p2
    te_ref[...] = jnp.where(lane == 0, i1, jnp.where(lane == 1, i2, 0.0)).astype(jnp.int32)
    tw_ref[...] = jnp.where(lane == 0, p1 / tot, jnp.where(lane == 1, p2 / tot, 0.0))


def _acc_ln_kernel(*refs, nk, emit_h, router):
    it = iter(refs)
    lhs_ref, w_ref, x_ref, gate_ref, lng_ref, lnb_ref = (next(it) for _ in range(6))
    sh_ref = sc_ref = wr_ref = ho_ref = te_ref = tw_ref = None
    if emit_h:
        sh_ref, sc_ref = next(it), next(it)
    if router:
        wr_ref = next(it)
    xo_ref = next(it)
    if emit_h:
        ho_ref = next(it)
    if router:
        te_ref, tw_ref = next(it), next(it)
    k = pl.program_id(1)
    part = _dot(lhs_ref[...], w_ref[...].astype(BF))

    @pl.when(k == 0)
    def _():
        xo_ref[...] = part

    @pl.when(k > 0)
    def _():
        xo_ref[...] += part

    @pl.when(k == nk - 1)
    def _():
        _residual_ln_epilogue(xo_ref[...], x_ref, gate_ref, lng_ref, lnb_ref, sh_ref, sc_ref, wr_ref,
                              xo_ref, ho_ref, te_ref, tw_ref)


def _once(shape, index):
    return pl.BlockSpec(shape, index, pipeline_mode=pl.Buffered(1))


def _acc_ln_call(lhs, w, w_layer, x, mods, gate_lnk, ln_g, ln_b, ln_lnk, next_mod, wr, wr_layer, name):
    tm, tk = 512, 512
    k_dim = lhs.shape[1]
    nk = k_dim // tk
    emit_h = next_mod is not None
    router = wr is not None
    gl, gn = gate_lnk
    ll, ln_n = ln_lnk
    row_once = _once((tm, D_MODEL), lambda i, k: (i, 0))
    vec = _once((None, None, 1, D_MODEL), lambda i, k: (ll, ln_n, 0, 0))

    def mod(l, n, kk):
        return _once((None, None, None, None, SLAB, D_MODEL),
                     lambda i, k: (l, n, kk, _group_of_tile(i, tm), 0, 0))

    in_specs = [
        pl.BlockSpec((tm, tk), lambda i, k: (i, k)),
        pl.BlockSpec((None, tk, D_MODEL), lambda i, k: (w_layer, k, 0)),
        row_once, mod(gl, gn, 2), vec, vec,
    ]
    args = [lhs, w, x, mods, ln_g.reshape(DEPTH, 2, 1, D_MODEL), ln_b.reshape(DEPTH, 2, 1, D_MODEL)]
    row_out = pl.BlockSpec((tm, D_MODEL), lambda i, k: (i, 0))
    out_specs = [row_out]
    out_shape = [jax.ShapeDtypeStruct((M_ROWS, D_MODEL), F32)]
    if emit_h:
        nl, nn = next_mod
        in_specs += [mod(nl, nn, 0), mod(nl, nn, 1)]
        args += [mods, mods]
        out_specs.append(row_out)
        out_shape.append(jax.ShapeDtypeStruct((M_ROWS, D_MODEL), BF))
    if router:
        in_specs.append(_once((None, D_MODEL, ROUTER_LANES), lambda i, k: (wr_layer, 0, 0)))
        args.append(wr)
        lane_out = pl.BlockSpec((tm, ROUTER_LANES), lambda i, k: (i, 0))
        out_specs += [lane_out, lane_out]
        out_shape += [jax.ShapeDtypeStruct((M_ROWS, ROUTER_LANES), jnp.int32),
                      jax.ShapeDtypeStruct((M_ROWS, ROUTER_LANES), F32)]
    return pl.pallas_call(
        functools.partial(_acc_ln_kernel, nk=nk, emit_h=emit_h, router=router),
        grid=(M_ROWS // tm, nk),
        in_specs=in_specs,
        out_specs=out_specs,
        out_shape=out_shape,
        compiler_params=_params(("arbitrary", "arbitrary"), 56),
        name=name,
    )(*args)


def _gather_kernel(idx_ref, src_hbm, out_ref, sem, *, rows):
    def start(r, carry):
        pltpu.make_async_copy(src_hbm.at[pl.ds(idx_ref[0, r], 1)], out_ref.at[pl.ds(r, 1)], sem).start()
        return carry

    lax.fori_loop(0, rows, start, 0)

    def wait(r, carry):
        pltpu.make_async_copy(src_hbm.at[pl.ds(0, 1)], out_ref.at[pl.ds(r, 1)], sem).wait()
        return carry

    lax.fori_loop(0, rows, wait, 0)


def _gather_call(src, idx):
    rows = MOE_BLK
    width = src.shape[1]
    nblk = idx.shape[0] // rows
    return pl.pallas_call(
        functools.partial(_gather_kernel, rows=rows),
        grid=(nblk,),
        in_specs=[
            pl.BlockSpec((None, 1, rows), lambda i: (i, 0, 0), memory_space=pltpu.SMEM),
            pl.BlockSpec(memory_space=pl.ANY),
        ],
        out_specs=pl.BlockSpec((rows, width), lambda i: (i, 0)),
        out_shape=jax.ShapeDtypeStruct((idx.shape[0], width), src.dtype),
        scratch_shapes=[pltpu.SemaphoreType.DMA(())],
        compiler_params=_params(("arbitrary",), 32),
        name="moe_gather",
    )(idx.reshape(nblk, 1, rows), src)


def _combine_kernel(*refs, rows, emit_h):
    it = iter(refs)
    d0_ref, d1_ref, y_hbm, tw_ref, x_ref, gate_ref, lng_ref, lnb_ref = (next(it) for _ in range(8))
    sh_ref = sc_ref = ho_ref = None
    if emit_h:
        sh_ref, sc_ref = next(it), next(it)
    xo_ref = next(it)
    if emit_h:
        ho_ref = next(it)
    buf0, buf1, sem = next(it), next(it), next(it)

    def start(r, carry):
        pltpu.make_async_copy(y_hbm.at[pl.ds(d0_ref[0, r], 1)], buf0.at[pl.ds(r, 1)], sem).start()
        pltpu.make_async_copy(y_hbm.at[pl.ds(d1_ref[0, r], 1)], buf1.at[pl.ds(r, 1)], sem).start()
        return carry

    lax.fori_loop(0, rows, start, 0)

    def wait(r, carry):
        pltpu.make_async_copy(y_hbm.at[pl.ds(0, 1)], buf0.at[pl.ds(r, 1)], sem).wait()
        pltpu.make_async_copy(y_hbm.at[pl.ds(0, 1)], buf1.at[pl.ds(r, 1)], sem).wait()
        return carry

    lax.fori_loop(0, rows, wait, 0)

    f = buf0[...] * tw_ref[:, 0:1] + buf1[...] * tw_ref[:, 1:2]
    _residual_ln_epilogue(f, x_ref, gate_ref, lng_ref, lnb_ref, sh_ref, sc_ref, None,
                          xo_ref, ho_ref, None, None)


def _combine_call(y, dest0, dest1, tw, x, mods, gate_lnk, ln_g, ln_b, ln_lnk, next_mod):
    rows = 256
    nblk = M_ROWS // rows
    emit_h = next_mod is not None
    gl, gn = gate_lnk
    ll, ln_n = ln_lnk
    idx_spec = pl.BlockSpec((None, 1, rows), lambda i: (i, 0, 0), memory_space=pltpu.SMEM)
    row = pl.BlockSpec((rows, D_MODEL), lambda i: (i, 0))
    vec = pl.BlockSpec((None, None, 1, D_MODEL), lambda i: (ll, ln_n, 0, 0))
    in_specs = [
        idx_spec, idx_spec,
        pl.BlockSpec(memory_space=pl.ANY),
        pl.BlockSpec((rows, ROUTER_LANES), lambda i: (i, 0)),
        row, _mod_spec(gl, gn, 2, rows, 1, 0), vec, vec,
    ]
    args = [dest0.reshape(nblk, 1, rows), dest1.reshape(nblk, 1, rows), y, tw, x, mods,
            ln_g.reshape(DEPTH, 2, 1, D_MODEL), ln_b.reshape(DEPTH, 2, 1, D_MODEL)]
    out_specs = [row]
    out_shape = [jax.ShapeDtypeStruct((M_ROWS, D_MODEL), F32)]
    if emit_h:
        nl, nn = next_mod
        in_specs += [_mod_spec(nl, nn, 0, rows, 1, 0), _mod_spec(nl, nn, 1, rows, 1, 0)]
        args += [mods, mods]
        out_specs.append(row)
        out_shape.append(jax.ShapeDtypeStruct((M_ROWS, D_MODEL), BF))
    return pl.pallas_call(
        functools.partial(_combine_kernel, rows=rows, emit_h=emit_h),
        grid=(nblk,),
        in_specs=in_specs,
        out_specs=out_specs,
        out_shape=out_shape,
        scratch_shapes=[pltpu.VMEM((rows, D_MODEL), F32), pltpu.VMEM((rows, D_MODEL), F32),
                        pltpu.SemaphoreType.DMA(())],
        compiler_params=_params(("arbitrary",), 40),
        name="moe_combine",
    )(*args)


def _dispatch_plan(te):
    flat_e = te[:, :TOP_K].reshape(-1)
    onehot = (flat_e[:, None] == jnp.arange(N_EXPERTS, dtype=jnp.int32)[None, :]).astype(jnp.int32)
    csum = jnp.cumsum(onehot, axis=0)
    rank = jnp.take_along_axis(csum, flat_e[:, None], axis=1)[:, 0] - 1
    counts = csum[-1]
    padded = (counts + MOE_BLK - 1) // MOE_BLK * MOE_BLK
    pad_end = jnp.cumsum(padded)
    pad_start = pad_end - padded
    dest = (pad_start[flat_e] + rank).astype(jnp.int32)
    tok_of_row = jnp.zeros((MOE_ROWS,), jnp.int32).at[dest].set(
        jnp.arange(N_ASSIGN, dtype=jnp.int32) // TOP_K)
    block_e = jnp.minimum(
        jnp.searchsorted(pad_end, jnp.arange(MOE_NB, dtype=jnp.int32) * MOE_BLK, side="right"),
        N_EXPERTS - 1).astype(jnp.int32)
    dest2 = dest.reshape(M_ROWS, TOP_K)
    return tok_of_row, block_e, dest2[:, 0], dest2[:, 1]


def kernel(x_prompt, x_sample, state_conv, state_pool, c_prompt, c_sample, ln_in_g, ln_in_b, w_ada, b_ada,
           w_mix_in, sgu_g, sgu_b, w_spatial, b_spatial, conv_w, conv_b, w_pool, pool_scale, w_branch, w_o,
           ln_g, ln_b, w_ffn_gate, w_ffn_up, w_ffn_down, w_router, w_exp_gate, w_exp_up, w_exp_down):
    n_dense = w_ffn_gate.shape[0]
    n_moe = w_router.shape[0]

    x_all = jnp.concatenate(
        [x_prompt.reshape(P_ROWS, D_MODEL), x_sample.transpose(1, 0, 2).reshape(S_ROWS, D_MODEL)], axis=0)
    c_all = jnp.concatenate(
        [c_prompt, c_sample, jnp.zeros((C_ROWS - BATCH - DEC_BATCH, D_MODEL), F32)], axis=0)

    ada = _ada_call(c_all, w_ada, b_ada)
    ada = ada.reshape(DEPTH, 2, C_ROWS, 3, D_MODEL).transpose(0, 1, 3, 2, 4)
    mods = jnp.concatenate([
        jnp.broadcast_to(ada[:, :, :, :BATCH, None, :], (DEPTH, 2, 3, BATCH, SLAB, D_MODEL)),
        ada[:, :, :, None, BATCH:BATCH + DEC_BATCH, :],
    ], axis=3)

    wsp_bf = w_spatial.astype(BF)
    bsp_full = jnp.repeat(b_spatial.transpose(0, 2, 1), SGU_HD, axis=-1)
    wv = jnp.repeat(w_spatial[:, :, :DEC_SEQ, :DEC_SEQ].transpose(0, 2, 3, 1), SGU_HD, axis=-1)
    bv = bsp_full[:, :DEC_SEQ]
    wpool_bf = w_pool.astype(BF)
    state_conv_t = state_conv.transpose(0, 2, 1, 3)
    state_pool_t = state_pool.transpose(0, 2, 1, 3)
    wr_pad = jnp.pad(w_router, ((0, 0), (0, 0), (0, ROUTER_LANES - N_EXPERTS)))
    zero_blocks = lambda n: jnp.zeros((n,), jnp.int32)

    x, h = _ln_in_call(x_all, ln_in_g, ln_in_b, mods)

    conv_p, conv_s, pool_p, pool_s, v_s = [], [], [], [], []
    for l in range(DEPTH):
        i = l // 2
        proj = _gmm_call(h, [w_mix_in.reshape(DEPTH, 1, D_MODEL, MIX_IN_COLS)], l, zero_blocks(M_ROWS // 1024),
                         bm=1024, tn=1024, swiglu=False, out_dtype=F32, vmem_mib=48, name="mix_in")
        br_p, cp, pp = _mid_prompt_call(proj, l, sgu_g, sgu_b, wsp_bf, bsp_full, conv_w, conv_b, wpool_bf,
                                        pool_scale)
        br_s, vs, cs, ps = _mid_sample_call(proj.reshape(M_ROWS // SLAB, SLAB, MIX_IN_COLS), l, state_conv_t,
                                            state_pool_t, sgu_g, sgu_b, wv, bv, conv_w, conv_b, wpool_bf,
                                            pool_scale)
        conv_p.append(cp[:, 8 - (CONV_W - 1):, :])
        pool_p.append(pp[:, 16 - POOL_BUF:, :])
        conv_s.append(cs.transpose(1, 0, 2))
        pool_s.append(ps.transpose(1, 0, 2))
        v_s.append(vs.transpose(1, 0, 2))
        br = jnp.concatenate([br_p, br_s.reshape(S_ROWS, N_BRANCH * MIX_W)], axis=0)
        merged = _merge_call(br, proj, w_branch, l)
        next_mod = (l + 1, 0) if l + 1 < DEPTH else None
        if l % 2 == 0:
            x, h2 = _acc_ln_call(merged, w_o, l, x, mods, (l, 0), ln_g, ln_b, (l, 0), (l, 1), None, 0,
                                 "out_proj")
            hid = _gmm_call(h2, [w_ffn_gate.reshape(n_dense, 1, D_MODEL, D_FF),
                                 w_ffn_up.reshape(n_dense, 1, D_MODEL, D_FF)], i, zero_blocks(M_ROWS // 1024),
                            bm=1024, tn=512, swiglu=True, out_dtype=BF, vmem_mib=48, name="ffn_up")
            outs = _acc_ln_call(hid, w_ffn_down, i, x, mods, (l, 1), ln_g, ln_b, (l, 1), next_mod, None, 0,
                                "ffn_down")
        else:
            x, h2, te, tw = _acc_ln_call(merged, w_o, l, x, mods, (l, 0), ln_g, ln_b, (l, 0), (l, 1),
                                         wr_pad, i, "out_proj_router")
            tok_of_row, block_e, dest0, dest1 = _dispatch_plan(te)
            h2_words = lax.bitcast_convert_type(h2.reshape(M_ROWS, D_MODEL // 2, 2), jnp.uint32)
            rows_words = _gather_call(h2_words, tok_of_row)
            rows = lax.bitcast_convert_type(rows_words, BF).reshape(MOE_ROWS, D_MODEL)
            hid = _gmm_call(rows, [w_exp_gate, w_exp_up], i, block_e,
                            bm=MOE_BLK, tn=512, swiglu=True, out_dtype=BF, vmem_mib=48, name="moe_up")
            y = _gmm_call(hid, [w_exp_down], i, block_e,
                          bm=MOE_BLK, tn=512, swiglu=False, out_dtype=F32, vmem_mib=56, name="moe_down")
            outs = _combine_call(y, dest0, dest1, tw, x, mods, (l, 1), ln_g, ln_b, (l, 1), next_mod)
        if next_mod is None:
            (x,) = outs
        else:
            x, h = outs

    y_prompt = x[:P_ROWS].reshape(BATCH, SEQ, D_MODEL)
    y_sample = x[P_ROWS:].reshape(DEC_SEQ, DEC_BATCH, D_MODEL).transpose(1, 0, 2)
    return (y_prompt, y_sample, jnp.stack(conv_p, 0), jnp.stack(conv_s, 0), jnp.stack(pool_p, 0),
            jnp.stack(pool_s, 0), jnp.stack(v_s, 0))
```

```python
import functools

import jax
import jax.numpy as jnp
from jax import lax
from jax.experimental import pallas as pl
from jax.experimental.pallas import tpu as pltpu

D_MODEL = 2048
BATCH = 4
SEQ = 2048
DEPTH = 4
DEC_BATCH = 128
DEC_SEQ = 8
PAST_LEN = 16384
MIX_W = 1024
CHUNK = 128
SGU_HEADS = 8
SGU_HD = MIX_W // SGU_HEADS
CONV_W = 3
POOL_WINDOWS = (2, 4, 8, 16)
POOL_GROUPS = 4
POOL_GD = MIX_W // POOL_GROUPS
POOL_BUF = 15
N_BRANCH = 3
MIX_IN_COLS = 6 * MIX_W + N_BRANCH * D_MODEL
D_FF = 5632
N_EXPERTS = 8
TOP_K = 2
E_FF = 7168
DEEPNORM_ALPHA = (2 * DEPTH) ** 0.25
LN_EPS = 1e-5

P_ROWS = BATCH * SEQ
S_ROWS = DEC_SEQ * DEC_BATCH
M_ROWS = P_ROWS + S_ROWS
SLAB = DEC_BATCH
C_ROWS = 136
ROUTER_LANES = 128
MOE_BLK = 512
N_ASSIGN = M_ROWS * TOP_K
MOE_ROWS = N_ASSIGN + N_EXPERTS * MOE_BLK
POST_ROWS = 256

BF = jnp.bfloat16
F32 = jnp.float32

_MIB = 1 << 20


def _params(semantics, vmem_mib):
    return pltpu.CompilerParams(dimension_semantics=semantics, vmem_limit_bytes=vmem_mib * _MIB)


def _dot(a, b):
    return jnp.dot(a, b, preferred_element_type=F32)


def _ln(x, g, b):
    mu = jnp.mean(x, axis=-1, keepdims=True)
    xc = x - mu
    var = jnp.mean(xc * xc, axis=-1, keepdims=True)
    return xc * lax.rsqrt(var + LN_EPS) * g + b


def _slab_bcast(fn, x, *slabs):
    rows, d = x.shape
    x3 = x.reshape(rows // SLAB, SLAB, d)
    return fn(x3, *[s[None] for s in slabs]).reshape(rows, d)


def _modulate(x, shift, scale):
    return _slab_bcast(lambda a, sh, sc: a * (1.0 + sc) + sh, x, shift, scale)


def _mod_spec(l, n, k, tm):
    return pl.BlockSpec((None, None, None, None, SLAB, D_MODEL),
                        lambda i: (l, n, k, jnp.minimum(i * tm // SEQ, BATCH), 0, 0))


def _ada_kernel(c_ref, w_ref, b_ref, o_ref):
    c = c_ref[...]
    s = (c * jax.nn.sigmoid(c)).astype(BF)
    o_ref[...] = _dot(s, w_ref[...].astype(BF)) + b_ref[...]


def _ada_call(c_all, w_ada, b_ada):
    tn = 1024
    per_part = D_MODEL // tn
    return pl.pallas_call(
        _ada_kernel,
        grid=(DEPTH, 2, 3 * per_part),
        in_specs=[
            pl.BlockSpec((C_ROWS, D_MODEL), lambda l, n, j: (0, 0)),
            pl.BlockSpec((None, None, D_MODEL, tn), lambda l, n, j: (l, n, 0, j)),
            pl.BlockSpec((None, None, 1, tn), lambda l, n, j: (l, n, 0, j)),
        ],
        out_specs=pl.BlockSpec((None, None, None, C_ROWS, tn),
                               lambda l, n, j: (l, n, j // per_part, 0, j % per_part)),
        out_shape=jax.ShapeDtypeStruct((DEPTH, 2, 3, C_ROWS, D_MODEL), F32),
        compiler_params=_params(("arbitrary",) * 3, 40),
        name="ada",
    )(c_all, w_ada, b_ada.reshape(DEPTH, 2, 1, 3 * D_MODEL))


def _ln_in_kernel(x_ref, g_ref, b_ref, sh_ref, sc_ref, xo_ref, ho_ref):
    xn = _ln(x_ref[...], g_ref[...], b_ref[...])
    xo_ref[...] = xn
    ho_ref[...] = _modulate(xn, sh_ref[...], sc_ref[...]).astype(BF)


def _ln_in_call(x_all, g, b, mods):
    tm = POST_ROWS
    row = pl.BlockSpec((tm, D_MODEL), lambda i: (i, 0))
    vec = pl.BlockSpec((1, D_MODEL), lambda i: (0, 0))
    return pl.pallas_call(
        _ln_in_kernel,
        grid=(M_ROWS // tm,),
        in_specs=[row, vec, vec, _mod_spec(0, 0, 0, tm), _mod_spec(0, 0, 1, tm)],
        out_specs=[row, row],
        out_shape=[jax.ShapeDtypeStruct((M_ROWS, D_MODEL), F32), jax.ShapeDtypeStruct((M_ROWS, D_MODEL), BF)],
        compiler_params=_params(("arbitrary",), 40),
        name="ln_in",
    )(x_all, g.reshape(1, D_MODEL), b.reshape(1, D_MODEL), mods, mods)


def _gmm_kernel(be_ref, nu_ref, lhs_ref, *refs, n_w, swiglu):
    w_refs = refs[:n_w]
    o_ref = refs[n_w]
    wb_refs = refs[n_w + 1:]
    b = pl.program_id(1)
    prev = be_ref[jnp.maximum(b - 1, 0)]
    changed = jnp.logical_or(b == 0, be_ref[b] != prev)

    @pl.when(changed)
    def _():
        for w_ref, wb_ref in zip(w_refs, wb_refs):
            wb_ref[...] = w_ref[...].astype(BF)

    @pl.when(b < nu_ref[0])
    def _():
        lhs = lhs_ref[...]
        if swiglu:
            a = _dot(lhs, wb_refs[0][...])
            u = _dot(lhs, wb_refs[1][...])
            o_ref[...] = (a * jax.nn.sigmoid(a) * u).astype(o_ref.dtype)
        else:
            o_ref[...] = _dot(lhs, wb_refs[0][...]).astype(o_ref.dtype)

    @pl.when(b >= nu_ref[0])
    def _():
        o_ref[...] = jnp.zeros_like(o_ref)


def _gmm_call(lhs, weights, w_layer, block_e, n_used, *, bm, tn, swiglu, out_dtype, vmem_mib, name):
    rows, k = lhs.shape
    n = weights[0].shape[-1]
    n_w = len(weights)
    w_spec = pl.BlockSpec((None, None, k, tn), lambda j, b, be, nu: (w_layer, be[b], 0, j))
    grid_spec = pltpu.PrefetchScalarGridSpec(
        num_scalar_prefetch=2,
        grid=(n // tn, rows // bm),
        in_specs=[pl.BlockSpec((bm, k), lambda j, b, be, nu: (jnp.minimum(b, nu[0] - 1), 0))] + [w_spec] * n_w,
        out_specs=pl.BlockSpec((bm, tn), lambda j, b, be, nu: (b, j)),
        scratch_shapes=[pltpu.VMEM((k, tn), BF)] * n_w,
    )
    return pl.pallas_call(
        functools.partial(_gmm_kernel, n_w=n_w, swiglu=swiglu),
        grid_spec=grid_spec,
        out_shape=jax.ShapeDtypeStruct((rows, n), out_dtype),
        compiler_params=_params(("arbitrary", "arbitrary"), vmem_mib),
        name=name,
    )(block_e, n_used, lhs, *weights)


def _dense_mm(lhs, w, w_layer, *, bm, tn, swiglu=False, out_dtype=F32, vmem_mib, name):
    nb = lhs.shape[0] // bm
    weights = [wi.reshape(wi.shape[0], 1, *wi.shape[1:]) for wi in w]
    return _gmm_call(lhs, weights, w_layer, jnp.zeros((nb,), jnp.int32), jnp.full((1,), nb, jnp.int32),
                     bm=bm, tn=tn, swiglu=swiglu, out_dtype=out_dtype, vmem_mib=vmem_mib, name=name)


def _mid_prompt_kernel(*refs, rows, tiles_per_seq):
    br_ref = refs[14]
    i = pl.program_id(0)

    @pl.when(i < P_ROWS // rows)
    def _():
        _mid_prompt_tile(*refs, rows=rows, tile_in_seq=i % tiles_per_seq)

    @pl.when(i >= P_ROWS // rows)
    def _():
        br_ref[...] = jnp.zeros_like(br_ref)


def _mid_prompt_tile(u_ref, v_ref, gb_ref, gc_ref, hb_ref, p_ref, sg_ref, sb_ref, wsp_ref, bsp_ref,
                     cw_ref, cb_ref, wpool_ref, pscale_ref,
                     br_ref, convo_ref, poolo_ref, zh_ref, ph_ref, *, rows, tile_in_seq):

    @pl.when(tile_in_seq == 0)
    def _():
        zh_ref[...] = jnp.zeros_like(zh_ref)
        ph_ref[...] = jnp.zeros_like(ph_ref)

    v = _ln(v_ref[...], sg_ref[...], sb_ref[...])
    vb = v.astype(BF)
    r_id = lax.broadcasted_iota(jnp.int32, (CHUNK, CHUNK), 0)
    c_id = lax.broadcasted_iota(jnp.int32, (CHUNK, CHUNK), 1)
    causal = c_id <= r_id
    w_heads = [jnp.where(causal, wsp_ref[h], jnp.zeros((CHUNK, CHUNK), BF)) for h in range(SGU_HEADS)]
    chunks = []
    for c in range(rows // CHUNK):
        heads = [
            _dot(w_heads[h], vb[c * CHUNK:(c + 1) * CHUNK, h * SGU_HD:(h + 1) * SGU_HD])
            for h in range(SGU_HEADS)
        ]
        chunks.append(jnp.concatenate(heads, axis=1) + bsp_ref[...])
    s = jnp.concatenate(chunks, axis=0)
    br_ref[:, 0:MIX_W] = (u_ref[...] * s).astype(BF)

    z = gc_ref[...] * hb_ref[...]
    ze = jnp.concatenate([zh_ref[...], z], axis=0)
    z1 = pltpu.roll(ze, 1, 0)[8:]
    z2 = pltpu.roll(ze, 2, 0)[8:]
    conv = cb_ref[...] + z2 * cw_ref[0:1, :]
    conv = conv + z1 * cw_ref[1:2, :]
    conv = conv + z * cw_ref[2:3, :]
    br_ref[:, MIX_W:2 * MIX_W] = (gb_ref[...] * conv).astype(BF)

    p = p_ref[...]
    pe = jnp.concatenate([ph_ref[...], p], axis=0)
    win = pe + pltpu.roll(pe, 1, 0)
    sums = [win[16:, 0:POOL_GD]]
    shift = 2
    for j in range(1, POOL_GROUPS):
        win = win[:, POOL_GD:]
        win = win + pltpu.roll(win, shift, 0)
        sums.append(win[16:, 0:POOL_GD])
        shift *= 2
    pos1 = (tile_in_seq * rows + 1 + lax.broadcasted_iota(jnp.int32, (rows, 1), 0)).astype(F32)
    mixed = []
    for j, w in enumerate(POOL_WINDOWS):
        mean = sums[j] / jnp.minimum(float(w), pos1)
        pooled = mean - p[:, j * POOL_GD:(j + 1) * POOL_GD]
        mixed.append(_dot(pooled.astype(BF), wpool_ref[j]))
    y_c = jnp.concatenate(mixed, axis=1) * pscale_ref[...]
    br_ref[:, 2 * MIX_W:3 * MIX_W] = y_c.astype(BF)

    z_tail = z[rows - 8:, :]
    p_tail = p[rows - 16:, :]
    zh_ref[...] = z_tail
    ph_ref[...] = p_tail
    convo_ref[...] = z_tail
    poolo_ref[...] = p_tail


def _mid_prompt_call(proj, l, sgu_g, sgu_b, wsp_bf, bsp_full, conv_w, conv_b, wpool_bf, pool_scale):
    rows = 256
    tiles_per_seq = SEQ // rows

    def col(c):
        return pl.BlockSpec((rows, MIX_W), lambda i, c=c: (i, c))

    vec = pl.BlockSpec((None, 1, MIX_W), lambda i: (l, 0, 0))
    in_specs = [col(c) for c in range(6)] + [
        vec, vec,
        pl.BlockSpec((None, SGU_HEADS, CHUNK, CHUNK), lambda i: (l, 0, 0, 0)),
        pl.BlockSpec((None, CHUNK, MIX_W), lambda i: (l, 0, 0)),
        pl.BlockSpec((None, CONV_W, MIX_W), lambda i: (l, 0, 0)),
        vec,
        pl.BlockSpec((None, POOL_GROUPS, POOL_GD, POOL_GD), lambda i: (l, 0, 0, 0)),
        vec,
    ]
    def seq_of(i):
        return jnp.minimum(i // tiles_per_seq, BATCH - 1)

    out_specs = [
        pl.BlockSpec((rows, N_BRANCH * MIX_W), lambda i: (i, 0)),
        pl.BlockSpec((None, 8, MIX_W), lambda i: (seq_of(i), 0, 0)),
        pl.BlockSpec((None, 16, MIX_W), lambda i: (seq_of(i), 0, 0)),
    ]
    out_shape = [
        jax.ShapeDtypeStruct((M_ROWS, N_BRANCH * MIX_W), BF),
        jax.ShapeDtypeStruct((BATCH, 8, MIX_W), F32),
        jax.ShapeDtypeStruct((BATCH, 16, MIX_W), F32),
    ]
    return pl.pallas_call(
        functools.partial(_mid_prompt_kernel, rows=rows, tiles_per_seq=tiles_per_seq),
        grid=(M_ROWS // rows,),
        in_specs=in_specs,
        out_specs=out_specs,
        out_shape=out_shape,
        scratch_shapes=[pltpu.VMEM((8, MIX_W), F32), pltpu.VMEM((16, MIX_W), F32)],
        compiler_params=_params(("arbitrary",), 48),
        name="mid_prompt",
    )(proj, proj, proj, proj, proj, proj, sgu_g.reshape(DEPTH, 1, MIX_W), sgu_b.reshape(DEPTH, 1, MIX_W),
      wsp_bf, bsp_full, conv_w, conv_b.reshape(DEPTH, 1, MIX_W), wpool_bf, pool_scale.reshape(DEPTH, 1, MIX_W))


def _mid_sample_kernel(u_ref, v_ref, gb_ref, gc_ref, hb_ref, p_ref, cs_ref, ps_ref, sg_ref, sb_ref, wv_ref,
                       bv_ref, cw_ref, cb_ref, wpool_ref, pscale_ref, br_in_ref,
                       br_ref, vo_ref, convo_ref, poolo_ref, *, bs):
    del br_in_ref
    sg = sg_ref[...]
    sb = sb_ref[...]
    v = [_ln(v_ref[t], sg, sb) for t in range(DEC_SEQ)]
    for t in range(DEC_SEQ):
        vo_ref[t] = v[t]
        acc = v[0] * wv_ref[t, 0:1, :]
        for s in range(1, t + 1):
            acc = acc + v[s] * wv_ref[t, s:s + 1, :]
        br_ref[t, :, 0:MIX_W] = (u_ref[t] * (acc + bv_ref[t:t + 1, :])).astype(BF)

    zc = [cs_ref[0], cs_ref[1]] + [gc_ref[t] * hb_ref[t] for t in range(DEC_SEQ)]
    for t in range(DEC_SEQ):
        conv = cb_ref[...] + zc[t] * cw_ref[0:1, :]
        conv = conv + zc[t + 1] * cw_ref[1:2, :]
        conv = conv + zc[t + 2] * cw_ref[2:3, :]
        br_ref[t, :, MIX_W:2 * MIX_W] = (gb_ref[t] * conv).astype(BF)
    convo_ref[0] = zc[DEC_SEQ]
    convo_ref[1] = zc[DEC_SEQ + 1]

    pc = [ps_ref[i] for i in range(POOL_BUF)] + [p_ref[t] for t in range(DEC_SEQ)]
    for i in range(POOL_BUF):
        poolo_ref[i] = pc[DEC_SEQ + i]
    for j, w in enumerate(POOL_WINDOWS):
        lanes = slice(j * POOL_GD, (j + 1) * POOL_GD)
        pooled = []
        for t in range(DEC_SEQ):
            end = POOL_BUF + t
            tot = pc[end][:, lanes]
            for k in range(1, w):
                tot = tot + pc[end - k][:, lanes]
            cnt = float(min(w, PAST_LEN + t + 1))
            pooled.append(tot / cnt - pc[end][:, lanes])
        pooled = jnp.concatenate(pooled, axis=0).astype(BF)
        mixed = _dot(pooled, wpool_ref[j]) * pscale_ref[:, lanes]
        for t in range(DEC_SEQ):
            br_ref[t, :, 2 * MIX_W + j * POOL_GD:2 * MIX_W + (j + 1) * POOL_GD] = (
                mixed[t * bs:(t + 1) * bs].astype(BF))


def _mid_sample_call(proj3, br3, l, state_conv_t, state_pool_t, sgu_g, sgu_b, wv, bv, conv_w, conv_b, wpool_bf,
                     pool_scale):
    bs = 32
    t_block = P_ROWS // SLAB // DEC_SEQ
    n_slabs = M_ROWS // SLAB

    def col(c):
        return pl.BlockSpec((DEC_SEQ, bs, MIX_W), lambda i, c=c: (t_block, i, c))

    vec = pl.BlockSpec((None, 1, MIX_W), lambda i: (l, 0, 0))
    in_specs = [col(c) for c in range(6)] + [
        pl.BlockSpec((None, CONV_W - 1, bs, MIX_W), lambda i: (l, 0, i, 0)),
        pl.BlockSpec((None, POOL_BUF, bs, MIX_W), lambda i: (l, 0, i, 0)),
        vec, vec,
        pl.BlockSpec((None, DEC_SEQ, DEC_SEQ, MIX_W), lambda i: (l, 0, 0, 0)),
        pl.BlockSpec((None, DEC_SEQ, MIX_W), lambda i: (l, 0, 0)),
        pl.BlockSpec((None, CONV_W, MIX_W), lambda i: (l, 0, 0)),
        vec,
        pl.BlockSpec((None, POOL_GROUPS, POOL_GD, POOL_GD), lambda i: (l, 0, 0, 0)),
        vec,
        pl.BlockSpec(memory_space=pl.ANY),
    ]
    out_specs = [
        pl.BlockSpec((DEC_SEQ, bs, N_BRANCH * MIX_W), lambda i: (t_block, i, 0)),
        pl.BlockSpec((DEC_SEQ, bs, MIX_W), lambda i: (0, i, 0)),
        pl.BlockSpec((CONV_W - 1, bs, MIX_W), lambda i: (0, i, 0)),
        pl.BlockSpec((POOL_BUF, bs, MIX_W), lambda i: (0, i, 0)),
    ]
    out_shape = [
        jax.ShapeDtypeStruct((n_slabs, SLAB, N_BRANCH * MIX_W), BF),
        jax.ShapeDtypeStruct((DEC_SEQ, DEC_BATCH, MIX_W), F32),
        jax.ShapeDtypeStruct((CONV_W - 1, DEC_BATCH, MIX_W), F32),
        jax.ShapeDtypeStruct((POOL_BUF, DEC_BATCH, MIX_W), F32),
    ]
    return pl.pallas_call(
        functools.partial(_mid_sample_kernel, bs=bs),
        grid=(DEC_BATCH // bs,),
        in_specs=in_specs,
        out_specs=out_specs,
        out_shape=out_shape,
        input_output_aliases={len(in_specs) - 1: 0},
        compiler_params=_params(("arbitrary",), 48),
        name="mid_sample",
    )(proj3, proj3, proj3, proj3, proj3, proj3, state_conv_t, state_pool_t,
      sgu_g.reshape(DEPTH, 1, MIX_W), sgu_b.reshape(DEPTH, 1, MIX_W), wv, bv, conv_w,
      conv_b.reshape(DEPTH, 1, MIX_W), wpool_bf, pool_scale.reshape(DEPTH, 1, MIX_W), br3)


def _merge_kernel(br_ref, w_ref, g0_ref, g1_ref, g2_ref, o_ref, wb_ref):
    @pl.when(pl.program_id(1) == 0)
    def _():
        wb_ref[...] = w_ref[...].astype(BF)

    acc = None
    for r, g_ref in enumerate((g0_ref, g1_ref, g2_ref)):
        term = jax.nn.sigmoid(g_ref[...]) * _dot(br_ref[:, r * MIX_W:(r + 1) * MIX_W], wb_ref[r])
        acc = term if acc is None else acc + term
    o_ref[...] = acc.astype(BF)


def _merge_call(br, proj, w_branch, l):
    tm, tn = 512, 512
    gate_col0 = 6 * MIX_W // tn
    per_gate = D_MODEL // tn

    def gate(r):
        return pl.BlockSpec((tm, tn), lambda j, i, r=r: (i, gate_col0 + r * per_gate + j))

    return pl.pallas_call(
        _merge_kernel,
        grid=(D_MODEL // tn, M_ROWS // tm),
        in_specs=[
            pl.BlockSpec((tm, N_BRANCH * MIX_W), lambda j, i: (i, 0)),
            pl.BlockSpec((None, N_BRANCH, MIX_W, tn), lambda j, i: (l, 0, 0, j)),
            gate(0), gate(1), gate(2),
        ],
        out_specs=pl.BlockSpec((tm, tn), lambda j, i: (i, j)),
        out_shape=jax.ShapeDtypeStruct((M_ROWS, D_MODEL), BF),
        scratch_shapes=[pltpu.VMEM((N_BRANCH, MIX_W, tn), BF)],
        compiler_params=_params(("arbitrary", "arbitrary"), 48),
        name="merge",
    )(br, w_branch, proj, proj, proj)


def _router_top2(h, wr_ref, te_ref, tw_ref):
    rows = h.shape[0]
    logits = jnp.dot(h, wr_ref[...], preferred_element_type=F32, precision=lax.Precision.HIGHEST)
    lane = lax.broadcasted_iota(jnp.int32, (rows, ROUTER_LANES), 1)
    lane_f = lane.astype(F32)
    valid = lane < N_EXPERTS
    lg = jnp.where(valid, logits, -jnp.inf)
    ex = jnp.exp(lg - jnp.max(lg, axis=-1, keepdims=True))
    probs = ex / jnp.sum(ex, axis=-1, keepdims=True)
    cand = jnp.where(valid, probs, -1.0)
    p1 = jnp.max(cand, axis=-1, keepdims=True)
    i1 = jnp.min(jnp.where(cand == p1, lane_f, float(ROUTER_LANES)), axis=-1, keepdims=True)
    cand = jnp.where(lane_f == i1, -1.0, cand)
    p2 = jnp.max(cand, axis=-1, keepdims=True)
    i2 = jnp.min(jnp.where(cand == p2, lane_f, float(ROUTER_LANES)), axis=-1, keepdims=True)
    tot = p1 + p2
    te_ref[...] = jnp.where(lane == 0, i1, jnp.where(lane == 1, i2, 0.0)).astype(jnp.int32)
    tw_ref[...] = jnp.where(lane == 0, p1 / tot, jnp.where(lane == 1, p2 / tot, 0.0))


def _post_kernel(*refs, rows, nblk, gathered, emit_h, router):
    it = iter(refs)
    if gathered:
        d0c, d1c, d0n, d1n, y_hbm, tw_in_ref = (next(it) for _ in range(6))
    else:
        f_ref = next(it)
    x_ref, gate_ref, lng_ref, lnb_ref = (next(it) for _ in range(4))
    sh_ref = sc_ref = wr_ref = ho_ref = te_ref = tw_ref = None
    if emit_h:
        sh_ref, sc_ref = next(it), next(it)
    if router:
        wr_ref = next(it)
    xo_ref = next(it)
    if emit_h:
        ho_ref = next(it)
    if router:
        te_ref, tw_ref = next(it), next(it)

    if gathered:
        buf, sem = next(it), next(it)
        i = pl.program_id(0)
        slot = i % 2

        def issue(d0_ref, d1_ref, s):
            def body(r, carry):
                pltpu.make_async_copy(y_hbm.at[pl.ds(d0_ref[0, r], 1)], buf.at[s, 0, pl.ds(r, 1)], sem.at[s]).start()
                pltpu.make_async_copy(y_hbm.at[pl.ds(d1_ref[0, r], 1)], buf.at[s, 1, pl.ds(r, 1)], sem.at[s]).start()
                return carry

            lax.fori_loop(0, rows, body, 0)

        @pl.when(i == 0)
        def _():
            issue(d0c, d1c, 0)

        @pl.when(i + 1 < nblk)
        def _():
            issue(d0n, d1n, 1 - slot)

        def wait(r, carry):
            pltpu.make_async_copy(y_hbm.at[pl.ds(0, 1)], buf.at[slot, 0, pl.ds(r, 1)], sem.at[slot]).wait()
            pltpu.make_async_copy(y_hbm.at[pl.ds(0, 1)], buf.at[slot, 1, pl.ds(r, 1)], sem.at[slot]).wait()
            return carry

        lax.fori_loop(0, rows, wait, 0)
        f = buf[slot, 0] * tw_in_ref[:, 0:1] + buf[slot, 1] * tw_in_ref[:, 1:2]
    else:
        f = f_ref[...]

    y = _slab_bcast(lambda a, g: a * g, f, gate_ref[...]) + DEEPNORM_ALPHA * x_ref[...]
    xn = _ln(y, lng_ref[...], lnb_ref[...])
    xo_ref[...] = xn
    if emit_h:
        h = _modulate(xn, sh_ref[...], sc_ref[...])
        ho_ref[...] = h.astype(ho_ref.dtype)
        if router:
            _router_top2(h, wr_ref, te_ref, tw_ref)


def _post_call(src, x, mods, l, n, ln_g, ln_b, next_mod, *, h_dtype=BF, wr=None, wr_layer=0, gather=None, name):
    rows = POST_ROWS
    nblk = M_ROWS // rows
    emit_h = next_mod is not None
    router = wr is not None
    gathered = gather is not None
    row = pl.BlockSpec((rows, D_MODEL), lambda i: (i, 0))
    vec = pl.BlockSpec((None, None, 1, D_MODEL), lambda i: (l, n, 0, 0))
    in_specs, args, scratch = [], [], []
    if gathered:
        dest0, dest1, tw = gather
        cur = pl.BlockSpec((None, 1, rows), lambda i: (i, 0, 0), memory_space=pltpu.SMEM)
        nxt = pl.BlockSpec((None, 1, rows), lambda i: (jnp.minimum(i + 1, nblk - 1), 0, 0),
                           memory_space=pltpu.SMEM)
        d0, d1 = dest0.reshape(nblk, 1, rows), dest1.reshape(nblk, 1, rows)
        in_specs += [cur, cur, nxt, nxt, pl.BlockSpec(memory_space=pl.ANY),
                     pl.BlockSpec((rows, ROUTER_LANES), lambda i: (i, 0))]
        args += [d0, d1, d0, d1, src, tw]
        scratch = [pltpu.VMEM((2, TOP_K, rows, D_MODEL), F32), pltpu.SemaphoreType.DMA((2,))]
    else:
        in_specs.append(row)
        args.append(src)
    in_specs += [row, _mod_spec(l, n, 2, rows), vec, vec]
    args += [x, mods, ln_g.reshape(DEPTH, 2, 1, D_MODEL), ln_b.reshape(DEPTH, 2, 1, D_MODEL)]
    out_specs = [row]
    out_shape = [jax.ShapeDtypeStruct((M_ROWS, D_MODEL), F32)]
    if emit_h:
        nl, nn = next_mod
        in_specs += [_mod_spec(nl, nn, 0, rows), _mod_spec(nl, nn, 1, rows)]
        args += [mods, mods]
        out_specs.append(row)
        out_shape.append(jax.ShapeDtypeStruct((M_ROWS, D_MODEL), h_dtype))
    if router:
        in_specs.append(pl.BlockSpec((None, D_MODEL, ROUTER_LANES), lambda i: (wr_layer, 0, 0)))
        args.append(wr)
        lane_out = pl.BlockSpec((rows, ROUTER_LANES), lambda i: (i, 0))
        out_specs += [lane_out, lane_out]
        out_shape += [jax.ShapeDtypeStruct((M_ROWS, ROUTER_LANES), jnp.int32),
                      jax.ShapeDtypeStruct((M_ROWS, ROUTER_LANES), F32)]
    return pl.pallas_call(
        functools.partial(_post_kernel, rows=rows, nblk=nblk, gathered=gathered, emit_h=emit_h, router=router),
        grid=(nblk,),
        in_specs=in_specs,
        out_specs=out_specs,
        out_shape=out_shape,
        scratch_shapes=scratch,
        compiler_params=_params(("arbitrary",), 48),
        name=name,
    )(*args)


def _gather_kernel(idx_cur, idx_nxt, src_hbm, out_ref, buf, sem, *, rows, nblk):
    i = pl.program_id(0)
    slot = i % 2

    def issue(idx_ref, s):
        def body(r, carry):
            pltpu.make_async_copy(src_hbm.at[pl.ds(idx_ref[0, r], 1)], buf.at[s, pl.ds(r, 1)], sem.at[s]).start()
            return carry

        lax.fori_loop(0, rows, body, 0)

    @pl.when(i == 0)
    def _():
        issue(idx_cur, 0)

    @pl.when(i + 1 < nblk)
    def _():
        issue(idx_nxt, 1 - slot)

    def wait(r, carry):
        pltpu.make_async_copy(src_hbm.at[pl.ds(0, 1)], buf.at[slot, pl.ds(r, 1)], sem.at[slot]).wait()
        return carry

    lax.fori_loop(0, rows, wait, 0)
    out_ref[...] = buf[slot].astype(BF)


def _gather_call(src, idx):
    rows = 256
    nblk = idx.shape[0] // rows
    idx3 = idx.reshape(nblk, 1, rows)
    return pl.pallas_call(
        functools.partial(_gather_kernel, rows=rows, nblk=nblk),
        grid=(nblk,),
        in_specs=[
            pl.BlockSpec((None, 1, rows), lambda i: (i, 0, 0), memory_space=pltpu.SMEM),
            pl.BlockSpec((None, 1, rows), lambda i: (jnp.minimum(i + 1, nblk - 1), 0, 0),
                         memory_space=pltpu.SMEM),
            pl.BlockSpec(memory_space=pl.ANY),
        ],
        out_specs=pl.BlockSpec((rows, D_MODEL), lambda i: (i, 0)),
        out_shape=jax.ShapeDtypeStruct((idx.shape[0], D_MODEL), BF),
        scratch_shapes=[pltpu.VMEM((2, rows, D_MODEL), F32), pltpu.SemaphoreType.DMA((2,))],
        compiler_params=_params(("arbitrary",), 32),
        name="moe_gather",
    )(idx3, idx3, src)


def _dispatch_plan(te):
    flat_e = te[:, :TOP_K].reshape(-1)
    experts = jnp.arange(N_EXPERTS, dtype=jnp.int32)
    onehot = (flat_e[:, None] == experts[None, :]).astype(jnp.int32)
    csum = jnp.cumsum(onehot, axis=0)
    rank = jnp.sum(csum * onehot, axis=1) - 1
    counts = csum[-1]
    padded = (counts + MOE_BLK - 1) // MOE_BLK * MOE_BLK
    pad_end = jnp.cumsum(padded)
    pad_start = pad_end - padded
    dest = (jnp.sum(pad_start[None, :] * onehot, axis=1) + rank).astype(jnp.int32)
    tok_of_row = jnp.zeros((MOE_ROWS,), jnp.int32).at[dest].set(
        jnp.arange(N_ASSIGN, dtype=jnp.int32) // TOP_K)
    dest2 = dest.reshape(M_ROWS, TOP_K)

    def blocks(blk):
        starts = jnp.arange(MOE_ROWS // blk, dtype=jnp.int32) * blk
        n_used = (pad_end[-1] // blk).astype(jnp.int32)
        starts = jnp.minimum(starts, pad_end[-1] - blk)
        block_e = jnp.sum((pad_end[None, :] <= starts[:, None]).astype(jnp.int32), axis=1)
        return block_e.astype(jnp.int32), n_used.reshape(1)

    return tok_of_row, dest2[:, 0], dest2[:, 1], blocks


def kernel(x_prompt, x_sample, state_conv, state_pool, c_prompt, c_sample, ln_in_g, ln_in_b, w_ada, b_ada,
           w_mix_in, sgu_g, sgu_b, w_spatial, b_spatial, conv_w, conv_b, w_pool, pool_scale, w_branch, w_o,
           ln_g, ln_b, w_ffn_gate, w_ffn_up, w_ffn_down, w_router, w_exp_gate, w_exp_up, w_exp_down):
    x_all = jnp.concatenate(
        [x_prompt.reshape(P_ROWS, D_MODEL), x_sample.transpose(1, 0, 2).reshape(S_ROWS, D_MODEL)], axis=0)
    c_all = jnp.concatenate(
        [c_prompt, c_sample, jnp.zeros((C_ROWS - BATCH - DEC_BATCH, D_MODEL), F32)], axis=0)

    ada = _ada_call(c_all, w_ada, b_ada)
    mods = jnp.concatenate([
        jnp.broadcast_to(ada[:, :, :, :BATCH, None, :], (DEPTH, 2, 3, BATCH, SLAB, D_MODEL)),
        ada[:, :, :, None, BATCH:BATCH + DEC_BATCH, :],
    ], axis=3)

    wsp_bf = w_spatial.astype(BF)
    bsp_full = jnp.repeat(b_spatial.transpose(0, 2, 1), SGU_HD, axis=-1)
    wv = jnp.repeat(w_spatial[:, :, :DEC_SEQ, :DEC_SEQ].transpose(0, 2, 3, 1), SGU_HD, axis=-1)
    bv = bsp_full[:, :DEC_SEQ]
    wpool_bf = w_pool.astype(BF)
    state_conv_t = state_conv.transpose(0, 2, 1, 3)
    state_pool_t = state_pool.transpose(0, 2, 1, 3)
    wr_pad = jnp.pad(w_router, ((0, 0), (0, 0), (0, ROUTER_LANES - N_EXPERTS)))

    x, h = _ln_in_call(x_all, ln_in_g, ln_in_b, mods)

    conv_p, conv_s, pool_p, pool_s, v_s = [], [], [], [], []
    for l in range(DEPTH):
        i = l // 2
        proj = _dense_mm(h, [w_mix_in], l, bm=1024, tn=1024, vmem_mib=48, name="mix_in")
        br, cp, pp = _mid_prompt_call(proj, l, sgu_g, sgu_b, wsp_bf, bsp_full, conv_w, conv_b, wpool_bf,
                                      pool_scale)
        br3, vs, cs, ps = _mid_sample_call(
            proj.reshape(M_ROWS // SLAB, SLAB, MIX_IN_COLS), br.reshape(M_ROWS // SLAB, SLAB, N_BRANCH * MIX_W),
            l, state_conv_t, state_pool_t, sgu_g, sgu_b, wv, bv, conv_w, conv_b, wpool_bf, pool_scale)
        conv_p.append(cp[:, 8 - (CONV_W - 1):, :])
        pool_p.append(pp[:, 16 - POOL_BUF:, :])
        conv_s.append(cs.transpose(1, 0, 2))
        pool_s.append(ps.transpose(1, 0, 2))
        v_s.append(vs.transpose(1, 0, 2))
        merged = _merge_call(br3.reshape(M_ROWS, N_BRANCH * MIX_W), proj, w_branch, l)
        mix = _dense_mm(merged, [w_o], l, bm=1024, tn=1024, vmem_mib=48, name="out_proj")
        next_mod = (l + 1, 0) if l + 1 < DEPTH else None
        if l % 2 == 0:
            x, h2 = _post_call(mix, x, mods, l, 0, ln_g, ln_b, (l, 1), name="post_mix")
            hid = _dense_mm(h2, [w_ffn_gate, w_ffn_up], i, bm=1024, tn=512, swiglu=True, out_dtype=BF,
                            vmem_mib=48, name="ffn_up")
            f = _dense_mm(hid, [w_ffn_down], i, bm=512, tn=512, vmem_mib=52, name="ffn_down")
            outs = _post_call(f, x, mods, l, 1, ln_g, ln_b, next_mod, name="post_ffn")
        else:
            x, h2, te, tw = _post_call(mix, x, mods, l, 0, ln_g, ln_b, (l, 1), h_dtype=F32, wr=wr_pad,
                                       wr_layer=i, name="post_mix_router")
            tok_of_row, dest0, dest1, blocks = _dispatch_plan(te)
            rows = _gather_call(h2, tok_of_row)
            be_up, nu_up = blocks(MOE_BLK)
            hid = _gmm_call(rows, [w_exp_gate, w_exp_up], i, be_up, nu_up,
                            bm=MOE_BLK, tn=1024, swiglu=True, out_dtype=BF, vmem_mib=60, name="moe_up")
            be_dn, nu_dn = blocks(MOE_BLK // 2)
            y = _gmm_call(hid, [w_exp_down], i, be_dn, nu_dn,
                          bm=MOE_BLK // 2, tn=512, swiglu=False, out_dtype=F32, vmem_mib=56, name="moe_down")
            outs = _post_call(y, x, mods, l, 1, ln_g, ln_b, next_mod, gather=(dest0, dest1, tw),
                              name="moe_combine")
        if next_mod is None:
            (x,) = outs
        else:
            x, h = outs

    y_prompt = x[:P_ROWS].reshape(BATCH, SEQ, D_MODEL)
    y_sample = x[P_ROWS:].reshape(DEC_SEQ, DEC_BATCH, D_MODEL).transpose(1, 0, 2)
    return (y_prompt, y_sample, jnp.stack(conv_p, 0), jnp.stack(conv_s, 0), jnp.stack(pool_p, 0),
            jnp.stack(pool_s, 0), jnp.stack(v_s, 0))
```

```python
import functools

import jax
import jax.numpy as jnp
from jax import lax
from jax.experimental import pallas as pl
from jax.experimental.pallas import tpu as pltpu

D_MODEL = 2048
BATCH = 4
SEQ = 2048
DEPTH = 4
DEC_BATCH = 128
DEC_SEQ = 8
PAST_LEN = 16384
MIX_W = 1024
CHUNK = 128
SGU_HEADS = 8
SGU_HD = MIX_W // SGU_HEADS
CONV_W = 3
POOL_WINDOWS = (2, 4, 8, 16)
POOL_GROUPS = 4
POOL_GD = MIX_W // POOL_GROUPS
POOL_BUF = 15
N_BRANCH = 3
MIX_IN_COLS = 6 * MIX_W + N_BRANCH * D_MODEL
D_FF = 5632
N_EXPERTS = 8
TOP_K = 2
E_FF = 7168
DEEPNORM_ALPHA = (2 * DEPTH) ** 0.25
LN_EPS = 1e-5

P_ROWS = BATCH * SEQ
S_ROWS = DEC_SEQ * DEC_BATCH
M_ROWS = P_ROWS + S_ROWS
SLAB = DEC_BATCH
C_ROWS = 136
ROUTER_LANES = 128
MOE_BLK = 512
N_ASSIGN = M_ROWS * TOP_K
MOE_ROWS = N_ASSIGN + N_EXPERTS * MOE_BLK
POST_ROWS = 256
LANE_TILES = D_MODEL // 128

BF = jnp.bfloat16
F32 = jnp.float32

_MIB = 1 << 20


def _params(semantics, vmem_mib):
    return pltpu.CompilerParams(dimension_semantics=semantics, vmem_limit_bytes=vmem_mib * _MIB)


def _dot(a, b):
    return jnp.dot(a, b, preferred_element_type=F32)


def _ln(x, g, b):
    mu = jnp.mean(x, axis=-1, keepdims=True)
    xc = x - mu
    var = jnp.mean(xc * xc, axis=-1, keepdims=True)
    return xc * lax.rsqrt(var + LN_EPS) * g + b


def _slab_bcast(fn, x, *slabs):
    rows, d = x.shape
    x3 = x.reshape(rows // SLAB, SLAB, d)
    return fn(x3, *[s[None] for s in slabs]).reshape(rows, d)


def _modulate(x, shift, scale):
    return _slab_bcast(lambda a, sh, sc: a * (1.0 + sc) + sh, x, shift, scale)


def _mod_spec(l, n, k, tm):
    return pl.BlockSpec((None, None, None, None, SLAB, D_MODEL),
                        lambda i: (l, n, k, jnp.minimum(i * tm // SEQ, BATCH), 0, 0))


def _ada_kernel(c_ref, w_ref, b_ref, o_ref):
    c = c_ref[...]
    s = (c * jax.nn.sigmoid(c)).astype(BF)
    o_ref[...] = _dot(s, w_ref[...].astype(BF)) + b_ref[...]


def _ada_call(c_all, w_ada, b_ada):
    tn = 1024
    per_part = D_MODEL // tn
    return pl.pallas_call(
        _ada_kernel,
        grid=(DEPTH, 2, 3 * per_part),
        in_specs=[
            pl.BlockSpec((C_ROWS, D_MODEL), lambda l, n, j: (0, 0)),
            pl.BlockSpec((None, None, D_MODEL, tn), lambda l, n, j: (l, n, 0, j)),
            pl.BlockSpec((None, None, 1, tn), lambda l, n, j: (l, n, 0, j)),
        ],
        out_specs=pl.BlockSpec((None, None, None, C_ROWS, tn),
                               lambda l, n, j: (l, n, j // per_part, 0, j % per_part)),
        out_shape=jax.ShapeDtypeStruct((DEPTH, 2, 3, C_ROWS, D_MODEL), F32),
        compiler_params=_params(("arbitrary",) * 3, 40),
        name="ada",
    )(c_all, w_ada, b_ada.reshape(DEPTH, 2, 1, 3 * D_MODEL))


def _ln_in_kernel(x_ref, g_ref, b_ref, sh_ref, sc_ref, xo_ref, ho_ref):
    xn = _ln(x_ref[...], g_ref[...], b_ref[...])
    xo_ref[...] = xn
    ho_ref[...] = _modulate(xn, sh_ref[...], sc_ref[...]).astype(BF)


def _ln_in_call(x_all, g, b, mods):
    tm = POST_ROWS
    row = pl.BlockSpec((tm, D_MODEL), lambda i: (i, 0))
    vec = pl.BlockSpec((1, D_MODEL), lambda i: (0, 0))
    return pl.pallas_call(
        _ln_in_kernel,
        grid=(M_ROWS // tm,),
        in_specs=[row, vec, vec, _mod_spec(0, 0, 0, tm), _mod_spec(0, 0, 1, tm)],
        out_specs=[row, row],
        out_shape=[jax.ShapeDtypeStruct((M_ROWS, D_MODEL), F32), jax.ShapeDtypeStruct((M_ROWS, D_MODEL), BF)],
        compiler_params=_params(("arbitrary",), 40),
        name="ln_in",
    )(x_all, g.reshape(1, D_MODEL), b.reshape(1, D_MODEL), mods, mods)


def _gmm_kernel(be_ref, src_ref, live_ref, full_ref, lhs_ref, *refs, n_w, swiglu, halves):
    del src_ref
    w_refs = refs[:n_w]
    o_ref = refs[n_w]
    wb_refs = refs[n_w + 1:]
    b = pl.program_id(1)
    prev = be_ref[jnp.maximum(b - 1, 0)]
    changed = jnp.logical_or(b == 0, be_ref[b] != prev)

    @pl.when(changed)
    def _():
        for w_ref, wb_ref in zip(w_refs, wb_refs):
            wb_ref[...] = w_ref[...].astype(BF)

    def compute(lo, hi):
        lhs = lhs_ref[lo:hi, :]
        if swiglu:
            a = _dot(lhs, wb_refs[0][...])
            u = _dot(lhs, wb_refs[1][...])
            o_ref[lo:hi, :] = (a * jax.nn.sigmoid(a) * u).astype(o_ref.dtype)
        else:
            o_ref[lo:hi, :] = _dot(lhs, wb_refs[0][...]).astype(o_ref.dtype)

    bm = o_ref.shape[0]
    live = live_ref[b] == 1

    @pl.when(live)
    def _():
        if halves:
            half = bm // 2
            compute(0, half)

            @pl.when(full_ref[b] == 1)
            def _():
                compute(half, bm)

            @pl.when(full_ref[b] == 0)
            def _():
                o_ref[half:bm, :] = jnp.zeros((bm - half, o_ref.shape[1]), o_ref.dtype)
        else:
            compute(0, bm)

    @pl.when(jnp.logical_not(live))
    def _():
        o_ref[...] = jnp.zeros_like(o_ref)


def _gmm_call(lhs, weights, w_layer, plan, *, bm, tn, swiglu, halves, out_dtype, vmem_mib, name):
    rows, k = lhs.shape
    n = weights[0].shape[-1]
    n_w = len(weights)
    w_spec = pl.BlockSpec((None, None, k, tn), lambda j, b, be, sb, lv, fl: (w_layer, be[b], 0, j))
    grid_spec = pltpu.PrefetchScalarGridSpec(
        num_scalar_prefetch=4,
        grid=(n // tn, rows // bm),
        in_specs=[pl.BlockSpec((bm, k), lambda j, b, be, sb, lv, fl: (sb[b], 0))] + [w_spec] * n_w,
        out_specs=pl.BlockSpec((bm, tn), lambda j, b, be, sb, lv, fl: (b, j)),
        scratch_shapes=[pltpu.VMEM((k, tn), BF)] * n_w,
    )
    return pl.pallas_call(
        functools.partial(_gmm_kernel, n_w=n_w, swiglu=swiglu, halves=halves),
        grid_spec=grid_spec,
        out_shape=jax.ShapeDtypeStruct((rows, n), out_dtype),
        compiler_params=_params(("arbitrary", "arbitrary"), vmem_mib),
        name=name,
    )(*plan, lhs, *weights)


def _dense_mm(lhs, w, w_layer, *, bm, tn, swiglu=False, out_dtype=F32, vmem_mib, name):
    nb = lhs.shape[0] // bm
    ones = jnp.ones((nb,), jnp.int32)
    plan = (jnp.zeros((nb,), jnp.int32), jnp.arange(nb, dtype=jnp.int32), ones, ones)
    weights = [wi.reshape(wi.shape[0], 1, *wi.shape[1:]) for wi in w]
    return _gmm_call(lhs, weights, w_layer, plan, bm=bm, tn=tn, swiglu=swiglu, halves=False,
                     out_dtype=out_dtype, vmem_mib=vmem_mib, name=name)


def _mid_prompt_kernel(*refs, rows, tiles_per_seq):
    br_ref = refs[14]
    i = pl.program_id(0)

    @pl.when(i < P_ROWS // rows)
    def _():
        _mid_prompt_tile(*refs, rows=rows, tile_in_seq=i % tiles_per_seq)

    @pl.when(i >= P_ROWS // rows)
    def _():
        br_ref[...] = jnp.zeros_like(br_ref)


def _mid_prompt_tile(u_ref, v_ref, gb_ref, gc_ref, hb_ref, p_ref, sg_ref, sb_ref, wsp_ref, bsp_ref,
                     cw_ref, cb_ref, wpool_ref, pscale_ref,
                     br_ref, convo_ref, poolo_ref, zh_ref, ph_ref, *, rows, tile_in_seq):

    @pl.when(tile_in_seq == 0)
    def _():
        zh_ref[...] = jnp.zeros_like(zh_ref)
        ph_ref[...] = jnp.zeros_like(ph_ref)

    v = _ln(v_ref[...], sg_ref[...], sb_ref[...])
    vb = v.astype(BF)
    r_id = lax.broadcasted_iota(jnp.int32, (CHUNK, CHUNK), 0)
    c_id = lax.broadcasted_iota(jnp.int32, (CHUNK, CHUNK), 1)
    causal = c_id <= r_id
    w_heads = [jnp.where(causal, wsp_ref[h], jnp.zeros((CHUNK, CHUNK), BF)) for h in range(SGU_HEADS)]
    chunks = []
    for c in range(rows // CHUNK):
        heads = [
            _dot(w_heads[h], vb[c * CHUNK:(c + 1) * CHUNK, h * SGU_HD:(h + 1) * SGU_HD])
            for h in range(SGU_HEADS)
        ]
        chunks.append(jnp.concatenate(heads, axis=1) + bsp_ref[...])
    s = jnp.concatenate(chunks, axis=0)
    br_ref[:, 0:MIX_W] = (u_ref[...] * s).astype(BF)

    z = gc_ref[...] * hb_ref[...]
    ze = jnp.concatenate([zh_ref[...], z], axis=0)
    z1 = pltpu.roll(ze, 1, 0)[8:]
    z2 = pltpu.roll(ze, 2, 0)[8:]
    conv = cb_ref[...] + z2 * cw_ref[0:1, :]
    conv = conv + z1 * cw_ref[1:2, :]
    conv = conv + z * cw_ref[2:3, :]
    br_ref[:, MIX_W:2 * MIX_W] = (gb_ref[...] * conv).astype(BF)

    p = p_ref[...]
    pe = jnp.concatenate([ph_ref[...], p], axis=0)
    win = pe + pltpu.roll(pe, 1, 0)
    sums = [win[16:, 0:POOL_GD]]
    shift = 2
    for j in range(1, POOL_GROUPS):
        win = win[:, POOL_GD:]
        win = win + pltpu.roll(win, shift, 0)
        sums.append(win[16:, 0:POOL_GD])
        shift *= 2
    pos1 = (tile_in_seq * rows + 1 + lax.broadcasted_iota(jnp.int32, (rows, 1), 0)).astype(F32)
    mixed = []
    for j, w in enumerate(POOL_WINDOWS):
        mean = sums[j] / jnp.minimum(float(w), pos1)
        pooled = mean - p[:, j * POOL_GD:(j + 1) * POOL_GD]
        mixed.append(_dot(pooled.astype(BF), wpool_ref[j]))
    y_c = jnp.concatenate(mixed, axis=1) * pscale_ref[...]
    br_ref[:, 2 * MIX_W:3 * MIX_W] = y_c.astype(BF)

    z_tail = z[rows - 8:, :]
    p_tail = p[rows - 16:, :]
    zh_ref[...] = z_tail
    ph_ref[...] = p_tail
    convo_ref[...] = z_tail
    poolo_ref[...] = p_tail


def _mid_prompt_call(proj, l, sgu_g, sgu_b, wsp_bf, bsp_full, conv_w, conv_b, wpool_bf, pool_scale):
    rows = 256
    tiles_per_seq = SEQ // rows

    def col(c):
        return pl.BlockSpec((rows, MIX_W), lambda i, c=c: (i, c))

    vec = pl.BlockSpec((None, 1, MIX_W), lambda i: (l, 0, 0))
    in_specs = [col(c) for c in range(6)] + [
        vec, vec,
        pl.BlockSpec((None, SGU_HEADS, CHUNK, CHUNK), lambda i: (l, 0, 0, 0)),
        pl.BlockSpec((None, CHUNK, MIX_W), lambda i: (l, 0, 0)),
        pl.BlockSpec((None, CONV_W, MIX_W), lambda i: (l, 0, 0)),
        vec,
        pl.BlockSpec((None, POOL_GROUPS, POOL_GD, POOL_GD), lambda i: (l, 0, 0, 0)),
        vec,
    ]
    def seq_of(i):
        return jnp.minimum(i // tiles_per_seq, BATCH - 1)

    out_specs = [
        pl.BlockSpec((rows, N_BRANCH * MIX_W), lambda i: (i, 0)),
        pl.BlockSpec((None, 8, MIX_W), lambda i: (seq_of(i), 0, 0)),
        pl.BlockSpec((None, 16, MIX_W), lambda i: (seq_of(i), 0, 0)),
    ]
    out_shape = [
        jax.ShapeDtypeStruct((M_ROWS, N_BRANCH * MIX_W), BF),
        jax.ShapeDtypeStruct((BATCH, 8, MIX_W), F32),
        jax.ShapeDtypeStruct((BATCH, 16, MIX_W), F32),
    ]
    return pl.pallas_call(
        functools.partial(_mid_prompt_kernel, rows=rows, tiles_per_seq=tiles_per_seq),
        grid=(M_ROWS // rows,),
        in_specs=in_specs,
        out_specs=out_specs,
        out_shape=out_shape,
        scratch_shapes=[pltpu.VMEM((8, MIX_W), F32), pltpu.VMEM((16, MIX_W), F32)],
        compiler_params=_params(("arbitrary",), 48),
        name="mid_prompt",
    )(proj, proj, proj, proj, proj, proj, sgu_g.reshape(DEPTH, 1, MIX_W), sgu_b.reshape(DEPTH, 1, MIX_W),
      wsp_bf, bsp_full, conv_w, conv_b.reshape(DEPTH, 1, MIX_W), wpool_bf, pool_scale.reshape(DEPTH, 1, MIX_W))


def _mid_sample_kernel(u_ref, v_ref, gb_ref, gc_ref, hb_ref, p_ref, cs_ref, ps_ref, sg_ref, sb_ref, wv_ref,
                       bv_ref, cw_ref, cb_ref, wpool_ref, pscale_ref, br_in_ref,
                       br_ref, vo_ref, convo_ref, poolo_ref, *, bs):
    del br_in_ref
    sg = sg_ref[...]
    sb = sb_ref[...]
    v = [_ln(v_ref[t], sg, sb) for t in range(DEC_SEQ)]
    for t in range(DEC_SEQ):
        vo_ref[t] = v[t]
        acc = v[0] * wv_ref[t, 0:1, :]
        for s in range(1, t + 1):
            acc = acc + v[s] * wv_ref[t, s:s + 1, :]
        br_ref[t, :, 0:MIX_W] = (u_ref[t] * (acc + bv_ref[t:t + 1, :])).astype(BF)

    zc = [cs_ref[0], cs_ref[1]] + [gc_ref[t] * hb_ref[t] for t in range(DEC_SEQ)]
    for t in range(DEC_SEQ):
        conv = cb_ref[...] + zc[t] * cw_ref[0:1, :]
        conv = conv + zc[t + 1] * cw_ref[1:2, :]
        conv = conv + zc[t + 2] * cw_ref[2:3, :]
        br_ref[t, :, MIX_W:2 * MIX_W] = (gb_ref[t] * conv).astype(BF)
    convo_ref[0] = zc[DEC_SEQ]
    convo_ref[1] = zc[DEC_SEQ + 1]

    pc = [ps_ref[i] for i in range(POOL_BUF)] + [p_ref[t] for t in range(DEC_SEQ)]
    for i in range(POOL_BUF):
        poolo_ref[i] = pc[DEC_SEQ + i]
    for j, w in enumerate(POOL_WINDOWS):
        lanes = slice(j * POOL_GD, (j + 1) * POOL_GD)
        pooled = []
        for t in range(DEC_SEQ):
            end = POOL_BUF + t
            tot = pc[end][:, lanes]
            for k in range(1, w):
                tot = tot + pc[end - k][:, lanes]
            cnt = float(min(w, PAST_LEN + t + 1))
            pooled.append(tot / cnt - pc[end][:, lanes])
        pooled = jnp.concatenate(pooled, axis=0).astype(BF)
        mixed = _dot(pooled, wpool_ref[j]) * pscale_ref[:, lanes]
        for t in range(DEC_SEQ):
            br_ref[t, :, 2 * MIX_W + j * POOL_GD:2 * MIX_W + (j + 1) * POOL_GD] = (
                mixed[t * bs:(t + 1) * bs].astype(BF))


def _mid_sample_call(proj3, br3, l, state_conv_t, state_pool_t, sgu_g, sgu_b, wv, bv, conv_w, conv_b, wpool_bf,
                     pool_scale):
    bs = 32
    t_block = P_ROWS // SLAB // DEC_SEQ
    n_slabs = M_ROWS // SLAB

    def col(c):
        return pl.BlockSpec((DEC_SEQ, bs, MIX_W), lambda i, c=c: (t_block, i, c))

    vec = pl.BlockSpec((None, 1, MIX_W), lambda i: (l, 0, 0))
    in_specs = [col(c) for c in range(6)] + [
        pl.BlockSpec((None, CONV_W - 1, bs, MIX_W), lambda i: (l, 0, i, 0)),
        pl.BlockSpec((None, POOL_BUF, bs, MIX_W), lambda i: (l, 0, i, 0)),
        vec, vec,
        pl.BlockSpec((None, DEC_SEQ, DEC_SEQ, MIX_W), lambda i: (l, 0, 0, 0)),
        pl.BlockSpec((None, DEC_SEQ, MIX_W), lambda i: (l, 0, 0)),
        pl.BlockSpec((None, CONV_W, MIX_W), lambda i: (l, 0, 0)),
        vec,
        pl.BlockSpec((None, POOL_GROUPS, POOL_GD, POOL_GD), lambda i: (l, 0, 0, 0)),
        vec,
        pl.BlockSpec(memory_space=pl.ANY),
    ]
    out_specs = [
        pl.BlockSpec((DEC_SEQ, bs, N_BRANCH * MIX_W), lambda i: (t_block, i, 0)),
        pl.BlockSpec((DEC_SEQ, bs, MIX_W), lambda i: (0, i, 0)),
        pl.BlockSpec((CONV_W - 1, bs, MIX_W), lambda i: (0, i, 0)),
        pl.BlockSpec((POOL_BUF, bs, MIX_W), lambda i: (0, i, 0)),
    ]
    out_shape = [
        jax.ShapeDtypeStruct((n_slabs, SLAB, N_BRANCH * MIX_W), BF),
        jax.ShapeDtypeStruct((DEC_SEQ, DEC_BATCH, MIX_W), F32),
        jax.ShapeDtypeStruct((CONV_W - 1, DEC_BATCH, MIX_W), F32),
        jax.ShapeDtypeStruct((POOL_BUF, DEC_BATCH, MIX_W), F32),
    ]
    return pl.pallas_call(
        functools.partial(_mid_sample_kernel, bs=bs),
        grid=(DEC_BATCH // bs,),
        in_specs=in_specs,
        out_specs=out_specs,
        out_shape=out_shape,
        input_output_aliases={len(in_specs) - 1: 0},
        compiler_params=_params(("arbitrary",), 48),
        name="mid_sample",
    )(proj3, proj3, proj3, proj3, proj3, proj3, state_conv_t, state_pool_t,
      sgu_g.reshape(DEPTH, 1, MIX_W), sgu_b.reshape(DEPTH, 1, MIX_W), wv, bv, conv_w,
      conv_b.reshape(DEPTH, 1, MIX_W), wpool_bf, pool_scale.reshape(DEPTH, 1, MIX_W), br3)


def _merge_kernel(br_ref, w_ref, g0_ref, g1_ref, g2_ref, o_ref, wb_ref):
    @pl.when(pl.program_id(1) == 0)
    def _():
        wb_ref[...] = w_ref[...].astype(BF)

    acc = None
    for r, g_ref in enumerate((g0_ref, g1_ref, g2_ref)):
        term = jax.nn.sigmoid(g_ref[...]) * _dot(br_ref[:, r * MIX_W:(r + 1) * MIX_W], wb_ref[r])
        acc = term if acc is None else acc + term
    o_ref[...] = acc.astype(BF)


def _merge_call(br, proj, w_branch, l):
    tm, tn = 1024, 512
    gate_col0 = 6 * MIX_W // tn
    per_gate = D_MODEL // tn

    def gate(r):
        return pl.BlockSpec((tm, tn), lambda j, i, r=r: (i, gate_col0 + r * per_gate + j))

    return pl.pallas_call(
        _merge_kernel,
        grid=(D_MODEL // tn, M_ROWS // tm),
        in_specs=[
            pl.BlockSpec((tm, N_BRANCH * MIX_W), lambda j, i: (i, 0)),
            pl.BlockSpec((None, N_BRANCH, MIX_W, tn), lambda j, i: (l, 0, 0, j)),
            gate(0), gate(1), gate(2),
        ],
        out_specs=pl.BlockSpec((tm, tn), lambda j, i: (i, j)),
        out_shape=jax.ShapeDtypeStruct((M_ROWS, D_MODEL), BF),
        scratch_shapes=[pltpu.VMEM((N_BRANCH, MIX_W, tn), BF)],
        compiler_params=_params(("arbitrary", "arbitrary"), 56),
        name="merge",
    )(br, w_branch, proj, proj, proj)


def _router_top2(h, wr_ref, te_ref, tw_ref):
    rows = h.shape[0]
    logits = jnp.dot(h, wr_ref[...], preferred_element_type=F32, precision=lax.Precision.HIGHEST)
    lane = lax.broadcasted_iota(jnp.int32, (rows, ROUTER_LANES), 1)
    lane_f = lane.astype(F32)
    valid = lane < N_EXPERTS
    lg = jnp.where(valid, logits, -jnp.inf)
    ex = jnp.exp(lg - jnp.max(lg, axis=-1, keepdims=True))
    probs = ex / jnp.sum(ex, axis=-1, keepdims=True)
    cand = jnp.where(valid, probs, -1.0)
    p1 = jnp.max(cand, axis=-1, keepdims=True)
    i1 = jnp.min(jnp.where(cand == p1, lane_f, float(ROUTER_LANES)), axis=-1, keepdims=True)
    cand = jnp.where(lane_f == i1, -1.0, cand)
    p2 = jnp.max(cand, axis=-1, keepdims=True)
    i2 = jnp.min(jnp.where(cand == p2, lane_f, float(ROUTER_LANES)), axis=-1, keepdims=True)
    tot = p1 + p2
    te_ref[...] = jnp.where(lane == 0, i1, jnp.where(lane == 1, i2, 0.0)).astype(jnp.int32)
    tw_ref[...] = jnp.where(lane == 0, p1 / tot, jnp.where(lane == 1, p2 / tot, 0.0))


def _post_kernel(*refs, rows, nblk, gathered, projected, emit_h, router, h_token_major):
    it = iter(refs)
    if gathered:
        d0c, d1c, d0n, d1n, y_hbm, tw_in_ref = (next(it) for _ in range(6))
    else:
        f_ref = next(it)
    if projected:
        w_ref = next(it)
    x_ref, gate_ref, lng_ref, lnb_ref = (next(it) for _ in range(4))
    sh_ref = sc_ref = wr_ref = ho_ref = te_ref = tw_ref = None
    if emit_h:
        sh_ref, sc_ref = next(it), next(it)
    if router:
        wr_ref = next(it)
    xo_ref = next(it)
    if emit_h:
        ho_ref = next(it)
    if router:
        te_ref, tw_ref = next(it), next(it)

    i = pl.program_id(0)
    if projected:
        wb_ref = next(it)

        @pl.when(i == 0)
        def _():
            wb_ref[...] = w_ref[...].astype(BF)

    if gathered:
        buf, sem = next(it), next(it)
        slot = i % 2

        def issue(d0_ref, d1_ref, s):
            def body(r, carry):
                pltpu.make_async_copy(y_hbm.at[pl.ds(d0_ref[0, r], 1)], buf.at[s, 0, pl.ds(r, 1)], sem.at[s]).start()
                pltpu.make_async_copy(y_hbm.at[pl.ds(d1_ref[0, r], 1)], buf.at[s, 1, pl.ds(r, 1)], sem.at[s]).start()
                return carry

            lax.fori_loop(0, rows, body, 0)

        @pl.when(i == 0)
        def _():
            issue(d0c, d1c, 0)

        @pl.when(i + 1 < nblk)
        def _():
            issue(d0n, d1n, 1 - slot)

        def wait(r, carry):
            pltpu.make_async_copy(y_hbm.at[pl.ds(0, 1)], buf.at[slot, 0, pl.ds(r, 1)], sem.at[slot]).wait()
            pltpu.make_async_copy(y_hbm.at[pl.ds(0, 1)], buf.at[slot, 1, pl.ds(r, 1)], sem.at[slot]).wait()
            return carry

        lax.fori_loop(0, rows, wait, 0)
        f = buf[slot, 0] * tw_in_ref[:, 0:1] + buf[slot, 1] * tw_in_ref[:, 1:2]
    elif projected:
        f = _dot(f_ref[...], wb_ref[...])
    else:
        f = f_ref[...]

    y = _slab_bcast(lambda a, g: a * g, f, gate_ref[...]) + DEEPNORM_ALPHA * x_ref[...]
    xn = _ln(y, lng_ref[...], lnb_ref[...])
    xo_ref[...] = xn
    if emit_h:
        h = _modulate(xn, sh_ref[...], sc_ref[...])
        if h_token_major:
            for s in range(LANE_TILES):
                ho_ref[pl.ds(s, rows, stride=LANE_TILES), :] = h[:, s * 128:(s + 1) * 128]
        else:
            ho_ref[...] = h.astype(ho_ref.dtype)
        if router:
            _router_top2(h, wr_ref, te_ref, tw_ref)


def _post_call(src, x, mods, l, n, ln_g, ln_b, next_mod, *, h_token_major=False, wr=None, wr_layer=0,
               gather=None, proj=None, name):
    rows = POST_ROWS
    nblk = M_ROWS // rows
    emit_h = next_mod is not None
    router = wr is not None
    gathered = gather is not None
    projected = proj is not None
    row = pl.BlockSpec((rows, D_MODEL), lambda i: (i, 0))
    vec = pl.BlockSpec((None, None, 1, D_MODEL), lambda i: (l, n, 0, 0))
    in_specs, args, scratch = [], [], []
    if projected:
        w, w_layer = proj
        in_specs += [row, pl.BlockSpec((None, D_MODEL, D_MODEL), lambda i: (w_layer, 0, 0),
                                       pipeline_mode=pl.Buffered(1))]
        args += [src, w]
        scratch = [pltpu.VMEM((D_MODEL, D_MODEL), BF)]
    if gathered:
        dest0, dest1, tw = gather
        cur = pl.BlockSpec((None, 1, rows), lambda i: (i, 0, 0), memory_space=pltpu.SMEM)
        nxt = pl.BlockSpec((None, 1, rows), lambda i: (jnp.minimum(i + 1, nblk - 1), 0, 0),
                           memory_space=pltpu.SMEM)
        d0, d1 = dest0.reshape(nblk, 1, rows), dest1.reshape(nblk, 1, rows)
        in_specs += [cur, cur, nxt, nxt, pl.BlockSpec(memory_space=pl.ANY),
                     pl.BlockSpec((rows, ROUTER_LANES), lambda i: (i, 0))]
        args += [d0, d1, d0, d1, src, tw]
        scratch = [pltpu.VMEM((2, TOP_K, rows, D_MODEL), F32), pltpu.SemaphoreType.DMA((2,))]
    elif not projected:
        in_specs.append(row)
        args.append(src)
    in_specs += [row, _mod_spec(l, n, 2, rows), vec, vec]
    args += [x, mods, ln_g.reshape(DEPTH, 2, 1, D_MODEL), ln_b.reshape(DEPTH, 2, 1, D_MODEL)]
    out_specs = [row]
    out_shape = [jax.ShapeDtypeStruct((M_ROWS, D_MODEL), F32)]
    if emit_h:
        nl, nn = next_mod
        in_specs += [_mod_spec(nl, nn, 0, rows), _mod_spec(nl, nn, 1, rows)]
        args += [mods, mods]
        if h_token_major:
            out_specs.append(pl.BlockSpec((rows * LANE_TILES, 128), lambda i: (i, 0)))
            out_shape.append(jax.ShapeDtypeStruct((M_ROWS * LANE_TILES, 128), F32))
        else:
            out_specs.append(row)
            out_shape.append(jax.ShapeDtypeStruct((M_ROWS, D_MODEL), BF))
    if router:
        in_specs.append(pl.BlockSpec((None, D_MODEL, ROUTER_LANES), lambda i: (wr_layer, 0, 0)))
        args.append(wr)
        lane_out = pl.BlockSpec((rows, ROUTER_LANES), lambda i: (i, 0))
        out_specs += [lane_out, lane_out]
        out_shape += [jax.ShapeDtypeStruct((M_ROWS, ROUTER_LANES), jnp.int32),
                      jax.ShapeDtypeStruct((M_ROWS, ROUTER_LANES), F32)]
    return pl.pallas_call(
        functools.partial(_post_kernel, rows=rows, nblk=nblk, gathered=gathered, projected=projected,
                          emit_h=emit_h, router=router, h_token_major=h_token_major),
        grid=(nblk,),
        in_specs=in_specs,
        out_specs=out_specs,
        out_shape=out_shape,
        scratch_shapes=scratch,
        compiler_params=_params(("arbitrary",), 56),
        name=name,
    )(*args)


def _gather_kernel(idx_cur, idx_nxt, src_hbm, out_ref, buf, sem, *, rows, nblk):
    i = pl.program_id(0)
    slot = i % 2

    def token_copy(tok, s, r):
        dst = pl.multiple_of((s * rows + r) * LANE_TILES, LANE_TILES)
        return pltpu.make_async_copy(src_hbm.at[tok], buf.at[pl.ds(dst, LANE_TILES)], sem.at[s])

    def issue(idx_ref, s):
        def body(r, carry):
            token_copy(idx_ref[0, r], s, r).start()
            return carry

        lax.fori_loop(0, rows, body, 0)

    @pl.when(i == 0)
    def _():
        issue(idx_cur, 0)

    @pl.when(i + 1 < nblk)
    def _():
        issue(idx_nxt, 1 - slot)

    def wait(r, carry):
        token_copy(0, slot, r).wait()
        return carry

    lax.fori_loop(0, rows, wait, 0)

    for static_slot in range(2):
        @pl.when(slot == static_slot)
        def _(static_slot=static_slot):
            base = static_slot * rows * LANE_TILES
            for s in range(LANE_TILES):
                out_ref[:, s * 128:(s + 1) * 128] = buf[pl.ds(base + s, rows, stride=LANE_TILES), :].astype(BF)


def _gather_call(src, idx):
    rows = 256
    nblk = idx.shape[0] // rows
    idx3 = idx.reshape(nblk, 1, rows)
    return pl.pallas_call(
        functools.partial(_gather_kernel, rows=rows, nblk=nblk),
        grid=(nblk,),
        in_specs=[
            pl.BlockSpec((None, 1, rows), lambda i: (i, 0, 0), memory_space=pltpu.SMEM),
            pl.BlockSpec((None, 1, rows), lambda i: (jnp.minimum(i + 1, nblk - 1), 0, 0),
                         memory_space=pltpu.SMEM),
            pl.BlockSpec(memory_space=pl.ANY),
        ],
        out_specs=pl.BlockSpec((rows, D_MODEL), lambda i: (i, 0)),
        out_shape=jax.ShapeDtypeStruct((idx.shape[0], D_MODEL), BF),
        scratch_shapes=[pltpu.VMEM((2 * rows * LANE_TILES, 128), F32), pltpu.SemaphoreType.DMA((2,))],
        compiler_params=_params(("arbitrary",), 32),
        name="moe_gather",
    )(idx3, idx3, src)


def _dispatch_plan(te):
    flat_e = te[:, :TOP_K].reshape(-1)
    experts = jnp.arange(N_EXPERTS, dtype=jnp.int32)
    onehot = (flat_e[:, None] == experts[None, :]).astype(jnp.int32)
    csum = jnp.cumsum(onehot, axis=0)
    rank = jnp.sum(csum * onehot, axis=1) - 1
    counts = csum[-1]
    padded = (counts + MOE_BLK - 1) // MOE_BLK * MOE_BLK
    pad_end = jnp.cumsum(padded)
    pad_start = pad_end - padded
    dest = (jnp.sum(pad_start[None, :] * onehot, axis=1) + rank).astype(jnp.int32)
    tok_of_row = jnp.zeros((MOE_ROWS,), jnp.int32).at[dest].set(
        jnp.arange(N_ASSIGN, dtype=jnp.int32) // TOP_K)
    dest2 = dest.reshape(M_ROWS, TOP_K)

    valid_end = pad_start + counts

    def plan(blk):
        ids = jnp.arange(MOE_ROWS // blk, dtype=jnp.int32)
        starts = ids * blk
        in_range = jnp.minimum(starts, pad_end[-1] - blk)
        block_e = jnp.sum((pad_end[None, :] <= in_range[:, None]).astype(jnp.int32), axis=1)
        onehot_e = (block_e[:, None] == experts[None, :]).astype(jnp.int32)
        valid = jnp.clip(jnp.sum(valid_end[None, :] * onehot_e, axis=1) - starts, 0, blk)
        live = (valid > 0).astype(jnp.int32)
        full = (valid > blk // 2).astype(jnp.int32)
        src = lax.cummax(jnp.where(live == 1, ids, 0), axis=0)
        return block_e.astype(jnp.int32), src.astype(jnp.int32), live, full

    return tok_of_row, dest2[:, 0], dest2[:, 1], plan


def kernel(x_prompt, x_sample, state_conv, state_pool, c_prompt, c_sample, ln_in_g, ln_in_b, w_ada, b_ada,
           w_mix_in, sgu_g, sgu_b, w_spatial, b_spatial, conv_w, conv_b, w_pool, pool_scale, w_branch, w_o,
           ln_g, ln_b, w_ffn_gate, w_ffn_up, w_ffn_down, w_router, w_exp_gate, w_exp_up, w_exp_down):
    x_all = jnp.concatenate(
        [x_prompt.reshape(P_ROWS, D_MODEL), x_sample.transpose(1, 0, 2).reshape(S_ROWS, D_MODEL)], axis=0)
    c_all = jnp.concatenate(
        [c_prompt, c_sample, jnp.zeros((C_ROWS - BATCH - DEC_BATCH, D_MODEL), F32)], axis=0)

    ada = _ada_call(c_all, w_ada, b_ada)
    mods = jnp.concatenate([
        jnp.broadcast_to(ada[:, :, :, :BATCH, None, :], (DEPTH, 2, 3, BATCH, SLAB, D_MODEL)),
        ada[:, :, :, None, BATCH:BATCH + DEC_BATCH, :],
    ], axis=3)

    wsp_bf = w_spatial.astype(BF)
    bsp_full = jnp.repeat(b_spatial.transpose(0, 2, 1), SGU_HD, axis=-1)
    wv = jnp.repeat(w_spatial[:, :, :DEC_SEQ, :DEC_SEQ].transpose(0, 2, 3, 1), SGU_HD, axis=-1)
    bv = bsp_full[:, :DEC_SEQ]
    wpool_bf = w_pool.astype(BF)
    state_conv_t = state_conv.transpose(0, 2, 1, 3)
    state_pool_t = state_pool.transpose(0, 2, 1, 3)
    wr_pad = jnp.pad(w_router, ((0, 0), (0, 0), (0, ROUTER_LANES - N_EXPERTS)))

    x, h = _ln_in_call(x_all, ln_in_g, ln_in_b, mods)

    conv_p, conv_s, pool_p, pool_s, v_s = [], [], [], [], []
    for l in range(DEPTH):
        i = l // 2
        proj = _dense_mm(h, [w_mix_in], l, bm=1024, tn=1024, vmem_mib=48, name="mix_in")
        br, cp, pp = _mid_prompt_call(proj, l, sgu_g, sgu_b, wsp_bf, bsp_full, conv_w, conv_b, wpool_bf,
                                      pool_scale)
        br3, vs, cs, ps = _mid_sample_call(
            proj.reshape(M_ROWS // SLAB, SLAB, MIX_IN_COLS), br.reshape(M_ROWS // SLAB, SLAB, N_BRANCH * MIX_W),
            l, state_conv_t, state_pool_t, sgu_g, sgu_b, wv, bv, conv_w, conv_b, wpool_bf, pool_scale)
        conv_p.append(cp[:, 8 - (CONV_W - 1):, :])
        pool_p.append(pp[:, 16 - POOL_BUF:, :])
        conv_s.append(cs.transpose(1, 0, 2))
        pool_s.append(ps.transpose(1, 0, 2))
        v_s.append(vs.transpose(1, 0, 2))
        merged = _merge_call(br3.reshape(M_ROWS, N_BRANCH * MIX_W), proj, w_branch, l)
        next_mod = (l + 1, 0) if l + 1 < DEPTH else None
        if l % 2 == 0:
            x, h2 = _post_call(merged, x, mods, l, 0, ln_g, ln_b, (l, 1), proj=(w_o, l), name="out_proj_post")
            hid = _dense_mm(h2, [w_ffn_gate, w_ffn_up], i, bm=1024, tn=512, swiglu=True, out_dtype=BF,
                            vmem_mib=48, name="ffn_up")
            f = _dense_mm(hid, [w_ffn_down], i, bm=512, tn=512, vmem_mib=52, name="ffn_down")
            outs = _post_call(f, x, mods, l, 1, ln_g, ln_b, next_mod, name="post_ffn")
        else:
            x, h2, te, tw = _post_call(merged, x, mods, l, 0, ln_g, ln_b, (l, 1), proj=(w_o, l),
                                       h_token_major=True, wr=wr_pad, wr_layer=i, name="out_proj_post_router")
            tok_of_row, dest0, dest1, plan = _dispatch_plan(te)
            rows = _gather_call(h2.reshape(M_ROWS, LANE_TILES, 128), tok_of_row)
            hid = _gmm_call(rows, [w_exp_gate, w_exp_up], i, plan(MOE_BLK), bm=MOE_BLK, tn=1024, swiglu=True,
                            halves=True, out_dtype=BF, vmem_mib=60, name="moe_up")
            y = _gmm_call(hid, [w_exp_down], i, plan(MOE_BLK // 2), bm=MOE_BLK // 2, tn=512, swiglu=False,
                          halves=False, out_dtype=F32, vmem_mib=56, name="moe_down")
            outs = _post_call(y, x, mods, l, 1, ln_g, ln_b, next_mod, gather=(dest0, dest1, tw),
                              name="moe_combine")
        if next_mod is None:
            (x,) = outs
        else:
            x, h = outs

    y_prompt = x[:P_ROWS].reshape(BATCH, SEQ, D_MODEL)
    y_sample = x[P_ROWS:].reshape(DEC_SEQ, DEC_BATCH, D_MODEL).transpose(1, 0, 2)
    return (y_prompt, y_sample, jnp.stack(conv_p, 0), jnp.stack(conv_s, 0), jnp.stack(pool_p, 0),
            jnp.stack(pool_s, 0), jnp.stack(v_s, 0))
```

```python
import functools

import jax
import jax.numpy as jnp
from jax import lax
from jax.experimental import pallas as pl
from jax.experimental.pallas import tpu as pltpu

D_MODEL = 2048
BATCH = 4
SEQ = 2048
DEPTH = 4
DEC_BATCH = 128
DEC_SEQ = 8
PAST_LEN = 16384
MIX_W = 1024
CHUNK = 128
SGU_HEADS = 8
SGU_HD = MIX_W // SGU_HEADS
CONV_W = 3
POOL_WINDOWS = (2, 4, 8, 16)
POOL_GROUPS = 4
POOL_GD = MIX_W // POOL_GROUPS
POOL_BUF = 15
N_BRANCH = 3
MIX_IN_COLS = 6 * MIX_W + N_BRANCH * D_MODEL
D_FF = 5632
N_EXPERTS = 8
TOP_K = 2
E_FF = 7168
DEEPNORM_ALPHA = (2 * DEPTH) ** 0.25
LN_EPS = 1e-5

P_ROWS = BATCH * SEQ
S_ROWS = DEC_SEQ * DEC_BATCH
M_ROWS = P_ROWS + S_ROWS
SLAB = DEC_BATCH
C_ROWS = 136
ROUTER_LANES = 128
MOE_BLK = 512
N_ASSIGN = M_ROWS * TOP_K
MOE_ROWS = N_ASSIGN + N_EXPERTS * MOE_BLK
POST_ROWS = 256
LANE_TILES = D_MODEL // 128

BF = jnp.bfloat16
F32 = jnp.float32

_MIB = 1 << 20


def _params(semantics, vmem_mib):
    return pltpu.CompilerParams(dimension_semantics=semantics, vmem_limit_bytes=vmem_mib * _MIB)


def _dot(a, b):
    return jnp.dot(a, b, preferred_element_type=F32)


def _ln(x, g, b):
    mu = jnp.mean(x, axis=-1, keepdims=True)
    xc = x - mu
    var = jnp.mean(xc * xc, axis=-1, keepdims=True)
    return xc * lax.rsqrt(var + LN_EPS) * g + b


def _slab_bcast(fn, x, *slabs):
    rows, d = x.shape
    x3 = x.reshape(rows // SLAB, SLAB, d)
    return fn(x3, *[s[None] for s in slabs]).reshape(rows, d)


def _modulate(x, shift, scale):
    return _slab_bcast(lambda a, sh, sc: a * (1.0 + sc) + sh, x, shift, scale)


def _mod_spec(l, n, k, tm):
    return pl.BlockSpec((None, None, None, None, SLAB, D_MODEL),
                        lambda i: (l, n, k, jnp.minimum(i * tm // SEQ, BATCH), 0, 0))


def _ada_kernel(c_ref, w_ref, b_ref, o_ref):
    c = c_ref[...]
    s = (c * jax.nn.sigmoid(c)).astype(BF)
    ada = _dot(s, w_ref[...].astype(BF)) + b_ref[...]
    for g in range(BATCH):
        o_ref[g] = jnp.broadcast_to(ada[DEC_BATCH + g:DEC_BATCH + g + 1, :], o_ref.shape[1:])
    o_ref[BATCH] = ada[0:DEC_BATCH, :]


def _ada_call(c_all, w_ada, b_ada):
    tn = 1024
    per_part = D_MODEL // tn
    return pl.pallas_call(
        _ada_kernel,
        grid=(DEPTH, 2, 3 * per_part),
        in_specs=[
            pl.BlockSpec((C_ROWS, D_MODEL), lambda l, n, j: (0, 0)),
            pl.BlockSpec((None, None, D_MODEL, tn), lambda l, n, j: (l, n, 0, j)),
            pl.BlockSpec((None, None, 1, tn), lambda l, n, j: (l, n, 0, j)),
        ],
        out_specs=pl.BlockSpec((None, None, None, BATCH + 1, SLAB, tn),
                               lambda l, n, j: (l, n, j // per_part, 0, 0, j % per_part)),
        out_shape=jax.ShapeDtypeStruct((DEPTH, 2, 3, BATCH + 1, SLAB, D_MODEL), F32),
        compiler_params=_params(("arbitrary",) * 3, 40),
        name="ada",
    )(c_all, w_ada, b_ada.reshape(DEPTH, 2, 1, 3 * D_MODEL))


def _ln_in_kernel(xp_ref, xs_ref, g_ref, b_ref, sh_ref, sc_ref, xo_ref, ho_ref, *, prompt_tiles):
    def emit(x_ref):
        xn = _ln(x_ref[...], g_ref[...], b_ref[...])
        xo_ref[...] = xn
        ho_ref[...] = _modulate(xn, sh_ref[...], sc_ref[...]).astype(BF)

    i = pl.program_id(0)
    pl.when(i < prompt_tiles)(lambda: emit(xp_ref))
    pl.when(i >= prompt_tiles)(lambda: emit(xs_ref))


def _ln_in_call(x_prompt_rows, x_sample_rows, g, b, mods):
    tm = POST_ROWS
    prompt_tiles = P_ROWS // tm
    row = pl.BlockSpec((tm, D_MODEL), lambda i: (i, 0))
    vec = pl.BlockSpec((1, D_MODEL), lambda i: (0, 0))
    return pl.pallas_call(
        functools.partial(_ln_in_kernel, prompt_tiles=prompt_tiles),
        grid=(M_ROWS // tm,),
        in_specs=[
            pl.BlockSpec((tm, D_MODEL), lambda i: (jnp.minimum(i, prompt_tiles - 1), 0)),
            pl.BlockSpec((tm, D_MODEL), lambda i: (jnp.maximum(i - prompt_tiles, 0), 0)),
            vec, vec, _mod_spec(0, 0, 0, tm), _mod_spec(0, 0, 1, tm)],
        out_specs=[row, row],
        out_shape=[jax.ShapeDtypeStruct((M_ROWS, D_MODEL), F32), jax.ShapeDtypeStruct((M_ROWS, D_MODEL), BF)],
        compiler_params=_params(("arbitrary",), 40),
        name="ln_in",
    )(x_prompt_rows, x_sample_rows, g.reshape(1, D_MODEL), b.reshape(1, D_MODEL), mods, mods)


def _gmm_kernel(be_ref, src_ref, live_ref, nxt_ref, lhs_ref, *refs, n_w, swiglu, w_layer, tn, nj):
    del src_ref
    w_hbm = refs[:n_w]
    o_ref = refs[n_w]
    stage = refs[n_w + 1:2 * n_w + 1]
    wb_refs = refs[2 * n_w + 1:3 * n_w + 1]
    sem = refs[3 * n_w + 1]
    j = pl.program_id(0)
    b = pl.program_id(1)

    def tile_copy(k, e, jj):
        col = pl.multiple_of(jj * tn, tn)
        return pltpu.make_async_copy(w_hbm[k].at[w_layer, e, :, pl.ds(col, tn)], stage[k], sem.at[k])

    def start_tiles(e, jj):
        for k in range(n_w):
            tile_copy(k, e, jj).start()

    e_here = be_ref[b]
    changed = jnp.logical_or(b == 0, e_here != be_ref[jnp.maximum(b - 1, 0)])

    @pl.when(jnp.logical_and(j == 0, b == 0))
    def _():
        start_tiles(e_here, 0)

    @pl.when(changed)
    def _():
        for k in range(n_w):
            tile_copy(k, e_here, j).wait()
            wb_refs[k][...] = stage[k][...].astype(BF)
        nxt = nxt_ref[b]

        @pl.when(nxt >= 0)
        def _():
            start_tiles(nxt, j)

        @pl.when(jnp.logical_and(nxt < 0, j + 1 < nj))
        def _():
            start_tiles(be_ref[0], j + 1)

    live = live_ref[b] == 1

    @pl.when(live)
    def _():
        lhs = lhs_ref[...]
        if swiglu:
            a = _dot(lhs, wb_refs[0][...])
            u = _dot(lhs, wb_refs[1][...])
            o_ref[...] = (a * jax.nn.sigmoid(a) * u).astype(o_ref.dtype)
        else:
            o_ref[...] = _dot(lhs, wb_refs[0][...]).astype(o_ref.dtype)

    @pl.when(jnp.logical_not(live))
    def _():
        o_ref[...] = jnp.zeros_like(o_ref)


def _gmm_call(lhs, weights, w_layer, plan, *, bm, tn, swiglu, out_dtype, vmem_mib, name):
    rows, k = lhs.shape
    n = weights[0].shape[-1]
    n_w = len(weights)
    nj = n // tn
    grid_spec = pltpu.PrefetchScalarGridSpec(
        num_scalar_prefetch=4,
        grid=(nj, rows // bm),
        in_specs=[pl.BlockSpec((bm, k), lambda j, b, be, sb, lv, nx: (sb[b], 0))]
        + [pl.BlockSpec(memory_space=pl.ANY)] * n_w,
        out_specs=pl.BlockSpec((bm, tn), lambda j, b, be, sb, lv, nx: (b, j)),
        scratch_shapes=[pltpu.VMEM((k, tn), F32)] * n_w + [pltpu.VMEM((k, tn), BF)] * n_w
        + [pltpu.SemaphoreType.DMA((n_w,))],
    )
    return pl.pallas_call(
        functools.partial(_gmm_kernel, n_w=n_w, swiglu=swiglu, w_layer=w_layer, tn=tn, nj=nj),
        grid_spec=grid_spec,
        out_shape=jax.ShapeDtypeStruct((rows, n), out_dtype),
        compiler_params=_params(("arbitrary", "arbitrary"), vmem_mib),
        name=name,
    )(*plan, lhs, *weights)


def _dense_mm(lhs, w, w_layer, *, bm, tn, swiglu=False, out_dtype=F32, vmem_mib, name):
    nb = lhs.shape[0] // bm
    plan = (jnp.zeros((nb,), jnp.int32), jnp.arange(nb, dtype=jnp.int32), jnp.ones((nb,), jnp.int32),
            jnp.full((nb,), -1, jnp.int32))
    weights = [wi.reshape(wi.shape[0], 1, *wi.shape[1:]) for wi in w]
    return _gmm_call(lhs, weights, w_layer, plan, bm=bm, tn=tn, swiglu=swiglu, out_dtype=out_dtype,
                     vmem_mib=vmem_mib, name=name)


def _mid_prompt_kernel(*refs, rows, tiles_per_seq):
    br_ref = refs[14]
    i = pl.program_id(0)

    @pl.when(i < P_ROWS // rows)
    def _():
        _mid_prompt_tile(*refs, rows=rows, tile_in_seq=i % tiles_per_seq)

    @pl.when(i >= P_ROWS // rows)
    def _():
        br_ref[...] = jnp.zeros_like(br_ref)


def _mid_prompt_tile(u_ref, v_ref, gb_ref, gc_ref, hb_ref, p_ref, sg_ref, sb_ref, wsp_ref, bsp_ref,
                     cw_ref, cb_ref, wpool_ref, pscale_ref,
                     br_ref, convo_ref, poolo_ref, zh_ref, ph_ref, *, rows, tile_in_seq):

    @pl.when(tile_in_seq == 0)
    def _():
        zh_ref[...] = jnp.zeros_like(zh_ref)
        ph_ref[...] = jnp.zeros_like(ph_ref)

    v = _ln(v_ref[...], sg_ref[...], sb_ref[...])
    vb = v.astype(BF)
    r_id = lax.broadcasted_iota(jnp.int32, (CHUNK, CHUNK), 0)
    c_id = lax.broadcasted_iota(jnp.int32, (CHUNK, CHUNK), 1)
    causal = c_id <= r_id
    w_heads = [jnp.where(causal, wsp_ref[h], jnp.zeros((CHUNK, CHUNK), BF)) for h in range(SGU_HEADS)]
    chunks = []
    for c in range(rows // CHUNK):
        heads = [
            _dot(w_heads[h], vb[c * CHUNK:(c + 1) * CHUNK, h * SGU_HD:(h + 1) * SGU_HD])
            for h in range(SGU_HEADS)
        ]
        chunks.append(jnp.concatenate(heads, axis=1) + bsp_ref[...])
    s = jnp.concatenate(chunks, axis=0)
    br_ref[:, 0:MIX_W] = (u_ref[...] * s).astype(BF)

    z = gc_ref[...] * hb_ref[...]
    ze = jnp.concatenate([zh_ref[...], z], axis=0)
    z1 = pltpu.roll(ze, 1, 0)[8:]
    z2 = pltpu.roll(ze, 2, 0)[8:]
    conv = cb_ref[...] + z2 * cw_ref[0:1, :]
    conv = conv + z1 * cw_ref[1:2, :]
    conv = conv + z * cw_ref[2:3, :]
    br_ref[:, MIX_W:2 * MIX_W] = (gb_ref[...] * conv).astype(BF)

    p = p_ref[...]
    pe = jnp.concatenate([ph_ref[...], p], axis=0)
    win = pe + pltpu.roll(pe, 1, 0)
    sums = [win[16:, 0:POOL_GD]]
    shift = 2
    for j in range(1, POOL_GROUPS):
        win = win[:, POOL_GD:]
        win = win + pltpu.roll(win, shift, 0)
        sums.append(win[16:, 0:POOL_GD])
        shift *= 2
    pos1 = (tile_in_seq * rows + 1 + lax.broadcasted_iota(jnp.int32, (rows, 1), 0)).astype(F32)
    mixed = []
    for j, w in enumerate(POOL_WINDOWS):
        mean = sums[j] / jnp.minimum(float(w), pos1)
        pooled = mean - p[:, j * POOL_GD:(j + 1) * POOL_GD]
        mixed.append(_dot(pooled.astype(BF), wpool_ref[j]))
    y_c = jnp.concatenate(mixed, axis=1) * pscale_ref[...]
    br_ref[:, 2 * MIX_W:3 * MIX_W] = y_c.astype(BF)

    z_tail = z[rows - 8:, :]
    p_tail = p[rows - 16:, :]
    zh_ref[...] = z_tail
    ph_ref[...] = p_tail
    convo_ref[...] = z_tail
    poolo_ref[...] = p_tail


def _mid_prompt_call(proj, l, sgu_g, sgu_b, wsp_bf, bsp_full, conv_w, conv_b, wpool_bf, pool_scale):
    rows = 256
    tiles_per_seq = SEQ // rows

    def col(c):
        return pl.BlockSpec((rows, MIX_W), lambda i, c=c: (i, c))

    vec = pl.BlockSpec((None, 1, MIX_W), lambda i: (l, 0, 0))
    in_specs = [col(c) for c in range(6)] + [
        vec, vec,
        pl.BlockSpec((None, SGU_HEADS, CHUNK, CHUNK), lambda i: (l, 0, 0, 0)),
        pl.BlockSpec((None, CHUNK, MIX_W), lambda i: (l, 0, 0)),
        pl.BlockSpec((None, CONV_W, MIX_W), lambda i: (l, 0, 0)),
        vec,
        pl.BlockSpec((None, POOL_GROUPS, POOL_GD, POOL_GD), lambda i: (l, 0, 0, 0)),
        vec,
    ]
    def seq_of(i):
        return jnp.minimum(i // tiles_per_seq, BATCH - 1)

    out_specs = [
        pl.BlockSpec((rows, N_BRANCH * MIX_W), lambda i: (i, 0)),
        pl.BlockSpec((None, 8, MIX_W), lambda i: (seq_of(i), 0, 0)),
        pl.BlockSpec((None, 16, MIX_W), lambda i: (seq_of(i), 0, 0)),
    ]
    out_shape = [
        jax.ShapeDtypeStruct((M_ROWS, N_BRANCH * MIX_W), BF),
        jax.ShapeDtypeStruct((BATCH, 8, MIX_W), F32),
        jax.ShapeDtypeStruct((BATCH, 16, MIX_W), F32),
    ]
    return pl.pallas_call(
        functools.partial(_mid_prompt_kernel, rows=rows, tiles_per_seq=tiles_per_seq),
        grid=(M_ROWS // rows,),
        in_specs=in_specs,
        out_specs=out_specs,
        out_shape=out_shape,
        scratch_shapes=[pltpu.VMEM((8, MIX_W), F32), pltpu.VMEM((16, MIX_W), F32)],
        compiler_params=_params(("arbitrary",), 48),
        name="mid_prompt",
    )(proj, proj, proj, proj, proj, proj, sgu_g.reshape(DEPTH, 1, MIX_W), sgu_b.reshape(DEPTH, 1, MIX_W),
      wsp_bf, bsp_full, conv_w, conv_b.reshape(DEPTH, 1, MIX_W), wpool_bf, pool_scale.reshape(DEPTH, 1, MIX_W))


def _mid_sample_kernel(u_ref, v_ref, gb_ref, gc_ref, hb_ref, p_ref, cs_ref, ps_ref, sg_ref, sb_ref, wv_ref,
                       bv_ref, cw_ref, cb_ref, wpool_ref, pscale_ref, br_in_ref,
                       br_ref, vo_ref, convo_ref, poolo_ref, *, bs):
    del br_in_ref
    sg = sg_ref[...]
    sb = sb_ref[...]
    v = [_ln(v_ref[t], sg, sb) for t in range(DEC_SEQ)]
    for t in range(DEC_SEQ):
        vo_ref[t] = v[t]
        acc = v[0] * wv_ref[t, 0:1, :]
        for s in range(1, t + 1):
            acc = acc + v[s] * wv_ref[t, s:s + 1, :]
        br_ref[t, :, 0:MIX_W] = (u_ref[t] * (acc + bv_ref[t:t + 1, :])).astype(BF)

    zc = [cs_ref[0], cs_ref[1]] + [gc_ref[t] * hb_ref[t] for t in range(DEC_SEQ)]
    for t in range(DEC_SEQ):
        conv = cb_ref[...] + zc[t] * cw_ref[0:1, :]
        conv = conv + zc[t + 1] * cw_ref[1:2, :]
        conv = conv + zc[t + 2] * cw_ref[2:3, :]
        br_ref[t, :, MIX_W:2 * MIX_W] = (gb_ref[t] * conv).astype(BF)
    convo_ref[0] = zc[DEC_SEQ]
    convo_ref[1] = zc[DEC_SEQ + 1]

    pc = [ps_ref[i] for i in range(POOL_BUF)] + [p_ref[t] for t in range(DEC_SEQ)]
    for i in range(POOL_BUF):
        poolo_ref[i] = pc[DEC_SEQ + i]
    for j, w in enumerate(POOL_WINDOWS):
        lanes = slice(j * POOL_GD, (j + 1) * POOL_GD)
        pooled = []
        for t in range(DEC_SEQ):
            end = POOL_BUF + t
            tot = pc[end][:, lanes]
            for k in range(1, w):
                tot = tot + pc[end - k][:, lanes]
            cnt = float(min(w, PAST_LEN + t + 1))
            pooled.append(tot / cnt - pc[end][:, lanes])
        pooled = jnp.concatenate(pooled, axis=0).astype(BF)
        mixed = _dot(pooled, wpool_ref[j]) * pscale_ref[:, lanes]
        for t in range(DEC_SEQ):
            br_ref[t, :, 2 * MIX_W + j * POOL_GD:2 * MIX_W + (j + 1) * POOL_GD] = (
                mixed[t * bs:(t + 1) * bs].astype(BF))


def _mid_sample_call(proj3, br3, l, state_conv_t, state_pool_t, sgu_g, sgu_b, wv, bv, conv_w, conv_b, wpool_bf,
                     pool_scale):
    bs = 32
    t_block = P_ROWS // SLAB // DEC_SEQ
    n_slabs = M_ROWS // SLAB

    def col(c):
        return pl.BlockSpec((DEC_SEQ, bs, MIX_W), lambda i, c=c: (t_block, i, c))

    vec = pl.BlockSpec((None, 1, MIX_W), lambda i: (l, 0, 0))
    in_specs = [col(c) for c in range(6)] + [
        pl.BlockSpec((None, CONV_W - 1, bs, MIX_W), lambda i: (l, 0, i, 0)),
        pl.BlockSpec((None, POOL_BUF, bs, MIX_W), lambda i: (l, 0, i, 0)),
        vec, vec,
        pl.BlockSpec((None, DEC_SEQ, DEC_SEQ, MIX_W), lambda i: (l, 0, 0, 0)),
        pl.BlockSpec((None, DEC_SEQ, MIX_W), lambda i: (l, 0, 0)),
        pl.BlockSpec((None, CONV_W, MIX_W), lambda i: (l, 0, 0)),
        vec,
        pl.BlockSpec((None, POOL_GROUPS, POOL_GD, POOL_GD), lambda i: (l, 0, 0, 0)),
        vec,
        pl.BlockSpec(memory_space=pl.ANY),
    ]
    out_specs = [
        pl.BlockSpec((DEC_SEQ, bs, N_BRANCH * MIX_W), lambda i: (t_block, i, 0)),
        pl.BlockSpec((DEC_SEQ, bs, MIX_W), lambda i: (0, i, 0)),
        pl.BlockSpec((CONV_W - 1, bs, MIX_W), lambda i: (0, i, 0)),
        pl.BlockSpec((POOL_BUF, bs, MIX_W), lambda i: (0, i, 0)),
    ]
    out_shape = [
        jax.ShapeDtypeStruct((n_slabs, SLAB, N_BRANCH * MIX_W), BF),
        jax.ShapeDtypeStruct((DEC_SEQ, DEC_BATCH, MIX_W), F32),
        jax.ShapeDtypeStruct((CONV_W - 1, DEC_BATCH, MIX_W), F32),
        jax.ShapeDtypeStruct((POOL_BUF, DEC_BATCH, MIX_W), F32),
    ]
    return pl.pallas_call(
        functools.partial(_mid_sample_kernel, bs=bs),
        grid=(DEC_BATCH // bs,),
        in_specs=in_specs,
        out_specs=out_specs,
        out_shape=out_shape,
        input_output_aliases={len(in_specs) - 1: 0},
        compiler_params=_params(("arbitrary",), 48),
        name="mid_sample",
    )(proj3, proj3, proj3, proj3, proj3, proj3, state_conv_t, state_pool_t,
      sgu_g.reshape(DEPTH, 1, MIX_W), sgu_b.reshape(DEPTH, 1, MIX_W), wv, bv, conv_w,
      conv_b.reshape(DEPTH, 1, MIX_W), wpool_bf, pool_scale.reshape(DEPTH, 1, MIX_W), br3)


def _merge_kernel(br_ref, w_ref, g0_ref, g1_ref, g2_ref, o_ref, wb_ref):
    @pl.when(pl.program_id(1) == 0)
    def _():
        wb_ref[...] = w_ref[...].astype(BF)

    acc = None
    for r, g_ref in enumerate((g0_ref, g1_ref, g2_ref)):
        term = jax.nn.sigmoid(g_ref[...]) * _dot(br_ref[:, r * MIX_W:(r + 1) * MIX_W], wb_ref[r])
        acc = term if acc is None else acc + term
    o_ref[...] = acc.astype(BF)


def _merge_call(br, proj, w_branch, l):
    tm, tn = 1024, 512
    gate_col0 = 6 * MIX_W // tn
    per_gate = D_MODEL // tn

    def gate(r):
        return pl.BlockSpec((tm, tn), lambda j, i, r=r: (i, gate_col0 + r * per_gate + j))

    return pl.pallas_call(
        _merge_kernel,
        grid=(D_MODEL // tn, M_ROWS // tm),
        in_specs=[
            pl.BlockSpec((tm, N_BRANCH * MIX_W), lambda j, i: (i, 0)),
            pl.BlockSpec((None, N_BRANCH, MIX_W, tn), lambda j, i: (l, 0, 0, j)),
            gate(0), gate(1), gate(2),
        ],
        out_specs=pl.BlockSpec((tm, tn), lambda j, i: (i, j)),
        out_shape=jax.ShapeDtypeStruct((M_ROWS, D_MODEL), BF),
        scratch_shapes=[pltpu.VMEM((N_BRANCH, MIX_W, tn), BF)],
        compiler_params=_params(("arbitrary", "arbitrary"), 56),
        name="merge",
    )(br, w_branch, proj, proj, proj)


def _router_top2(h, wr_ref, te_ref, tw_ref):
    rows = h.shape[0]
    logits = jnp.dot(h, wr_ref[...], preferred_element_type=F32, precision=lax.Precision.HIGHEST)
    lane = lax.broadcasted_iota(jnp.int32, (rows, ROUTER_LANES), 1)
    lane_f = lane.astype(F32)
    valid = lane < N_EXPERTS
    lg = jnp.where(valid, logits, -jnp.inf)
    ex = jnp.exp(lg - jnp.max(lg, axis=-1, keepdims=True))
    probs = ex / jnp.sum(ex, axis=-1, keepdims=True)
    cand = jnp.where(valid, probs, -1.0)
    p1 = jnp.max(cand, axis=-1, keepdims=True)
    i1 = jnp.min(jnp.where(cand == p1, lane_f, float(ROUTER_LANES)), axis=-1, keepdims=True)
    cand = jnp.where(lane_f == i1, -1.0, cand)
    p2 = jnp.max(cand, axis=-1, keepdims=True)
    i2 = jnp.min(jnp.where(cand == p2, lane_f, float(ROUTER_LANES)), axis=-1, keepdims=True)
    tot = p1 + p2
    te_ref[...] = jnp.where(lane == 0, i1, jnp.where(lane == 1, i2, 0.0)).astype(jnp.int32)
    tw_ref[...] = jnp.where(lane == 0, p1 / tot, jnp.where(lane == 1, p2 / tot, 0.0))


def _post_kernel(*refs, rows, nblk, gathered, projected, emit_h, router, h_token_major):
    it = iter(refs)
    if gathered:
        d0c, d1c, d0n, d1n, y_hbm, tw_in_ref = (next(it) for _ in range(6))
    else:
        f_ref = next(it)
    if projected:
        w_ref = next(it)
    x_ref, gate_ref, lng_ref, lnb_ref = (next(it) for _ in range(4))
    sh_ref = sc_ref = wr_ref = ho_ref = te_ref = tw_ref = None
    if emit_h:
        sh_ref, sc_ref = next(it), next(it)
    if router:
        wr_ref = next(it)
    xo_ref = next(it)
    xo_sample_ref = None if emit_h else next(it)
    if emit_h:
        ho_ref = next(it)
    if router:
        te_ref, tw_ref = next(it), next(it)

    i = pl.program_id(0)
    if projected:
        wb_ref = next(it)

        @pl.when(i == 0)
        def _():
            wb_ref[...] = w_ref[...].astype(BF)

    if gathered:
        buf, sem = next(it), next(it)
        slot = i % 2

        def issue(d0_ref, d1_ref, s):
            def body(r, carry):
                pltpu.make_async_copy(y_hbm.at[pl.ds(d0_ref[0, r], 1)], buf.at[s, 0, pl.ds(r, 1)], sem.at[s]).start()
                pltpu.make_async_copy(y_hbm.at[pl.ds(d1_ref[0, r], 1)], buf.at[s, 1, pl.ds(r, 1)], sem.at[s]).start()
                return carry

            lax.fori_loop(0, rows, body, 0)

        @pl.when(i == 0)
        def _():
            issue(d0c, d1c, 0)

        @pl.when(i + 1 < nblk)
        def _():
            issue(d0n, d1n, 1 - slot)

        def wait(r, carry):
            pltpu.make_async_copy(y_hbm.at[pl.ds(0, 1)], buf.at[slot, 0, pl.ds(r, 1)], sem.at[slot]).wait()
            pltpu.make_async_copy(y_hbm.at[pl.ds(0, 1)], buf.at[slot, 1, pl.ds(r, 1)], sem.at[slot]).wait()
            return carry

        lax.fori_loop(0, rows, wait, 0)
        f = buf[slot, 0] * tw_in_ref[:, 0:1] + buf[slot, 1] * tw_in_ref[:, 1:2]
    elif projected:
        f = _dot(f_ref[...], wb_ref[...])
    else:
        f = f_ref[...]

    y = _slab_bcast(lambda a, g: a * g, f, gate_ref[...]) + DEEPNORM_ALPHA * x_ref[...]
    xn = _ln(y, lng_ref[...], lnb_ref[...])
    if emit_h:
        xo_ref[...] = xn
    else:
        @pl.when(i < P_ROWS // rows)
        def _():
            xo_ref[...] = xn

        @pl.when(i >= P_ROWS // rows)
        def _():
            xo_sample_ref[...] = xn

    if emit_h:
        h = _modulate(xn, sh_ref[...], sc_ref[...])
        if h_token_major:
            for s in range(LANE_TILES):
                ho_ref[pl.ds(s, rows, stride=LANE_TILES), :] = h[:, s * 128:(s + 1) * 128]
        else:
            ho_ref[...] = h.astype(ho_ref.dtype)
        if router:
            _router_top2(h, wr_ref, te_ref, tw_ref)


def _post_call(src, x, mods, l, n, ln_g, ln_b, next_mod, *, h_token_major=False, wr=None, wr_layer=0,
               gather=None, proj=None, name):
    rows = POST_ROWS
    nblk = M_ROWS // rows
    emit_h = next_mod is not None
    router = wr is not None
    gathered = gather is not None
    projected = proj is not None
    row = pl.BlockSpec((rows, D_MODEL), lambda i: (i, 0))
    vec = pl.BlockSpec((None, None, 1, D_MODEL), lambda i: (l, n, 0, 0))
    in_specs, args, scratch = [], [], []
    if projected:
        w, w_layer = proj
        in_specs += [row, pl.BlockSpec((None, D_MODEL, D_MODEL), lambda i: (w_layer, 0, 0),
                                       pipeline_mode=pl.Buffered(1))]
        args += [src, w]
        scratch = [pltpu.VMEM((D_MODEL, D_MODEL), BF)]
    if gathered:
        dest0, dest1, tw = gather
        cur = pl.BlockSpec((None, 1, rows), lambda i: (i, 0, 0), memory_space=pltpu.SMEM)
        nxt = pl.BlockSpec((None, 1, rows), lambda i: (jnp.minimum(i + 1, nblk - 1), 0, 0),
                           memory_space=pltpu.SMEM)
        d0, d1 = dest0.reshape(nblk, 1, rows), dest1.reshape(nblk, 1, rows)
        in_specs += [cur, cur, nxt, nxt, pl.BlockSpec(memory_space=pl.ANY),
                     pl.BlockSpec((rows, ROUTER_LANES), lambda i: (i, 0))]
        args += [d0, d1, d0, d1, src, tw]
        scratch = [pltpu.VMEM((2, TOP_K, rows, D_MODEL), F32), pltpu.SemaphoreType.DMA((2,))]
    elif not projected:
        in_specs.append(row)
        args.append(src)
    in_specs += [row, _mod_spec(l, n, 2, rows), vec, vec]
    args += [x, mods, ln_g.reshape(DEPTH, 2, 1, D_MODEL), ln_b.reshape(DEPTH, 2, 1, D_MODEL)]
    if emit_h:
        out_specs = [row]
        out_shape = [jax.ShapeDtypeStruct((M_ROWS, D_MODEL), F32)]
    else:
        prompt_tiles = P_ROWS // rows
        out_specs = [pl.BlockSpec((rows, D_MODEL), lambda i: (jnp.minimum(i, prompt_tiles - 1), 0)),
                     pl.BlockSpec((rows, D_MODEL), lambda i: (jnp.maximum(i - prompt_tiles, 0), 0))]
        out_shape = [jax.ShapeDtypeStruct((P_ROWS, D_MODEL), F32), jax.ShapeDtypeStruct((S_ROWS, D_MODEL), F32)]
    if emit_h:
        nl, nn = next_mod
        in_specs += [_mod_spec(nl, nn, 0, rows), _mod_spec(nl, nn, 1, rows)]
        args += [mods, mods]
        if h_token_major:
            out_specs.append(pl.BlockSpec((rows * LANE_TILES, 128), lambda i: (i, 0)))
            out_shape.append(jax.ShapeDtypeStruct((M_ROWS * LANE_TILES, 128), F32))
        else:
            out_specs.append(row)
            out_shape.append(jax.ShapeDtypeStruct((M_ROWS, D_MODEL), BF))
    if router:
        in_specs.append(pl.BlockSpec((None, D_MODEL, ROUTER_LANES), lambda i: (wr_layer, 0, 0)))
        args.append(wr)
        lane_out = pl.BlockSpec((rows, ROUTER_LANES), lambda i: (i, 0))
        out_specs += [lane_out, lane_out]
        out_shape += [jax.ShapeDtypeStruct((M_ROWS, ROUTER_LANES), jnp.int32),
                      jax.ShapeDtypeStruct((M_ROWS, ROUTER_LANES), F32)]
    return pl.pallas_call(
        functools.partial(_post_kernel, rows=rows, nblk=nblk, gathered=gathered, projected=projected,
                          emit_h=emit_h, router=router, h_token_major=h_token_major),
        grid=(nblk,),
        in_specs=in_specs,
        out_specs=out_specs,
        out_shape=out_shape,
        scratch_shapes=scratch,
        compiler_params=_params(("arbitrary",), 56),
        name=name,
    )(*args)


def _gather_kernel(idx_cur, idx_nxt, src_hbm, out_ref, buf, sem, *, rows, nblk):
    i = pl.program_id(0)
    slot = i % 2

    def token_copy(tok, s, r):
        dst = pl.multiple_of((s * rows + r) * LANE_TILES, LANE_TILES)
        return pltpu.make_async_copy(src_hbm.at[tok], buf.at[pl.ds(dst, LANE_TILES)], sem.at[s])

    def issue(idx_ref, s):
        def body(r, carry):
            token_copy(idx_ref[0, r], s, r).start()
            return carry

        lax.fori_loop(0, rows, body, 0)

    @pl.when(i == 0)
    def _():
        issue(idx_cur, 0)

    @pl.when(i + 1 < nblk)
    def _():
        issue(idx_nxt, 1 - slot)

    def wait(r, carry):
        token_copy(0, slot, r).wait()
        return carry

    lax.fori_loop(0, rows, wait, 0)

    for static_slot in range(2):
        @pl.when(slot == static_slot)
        def _(static_slot=static_slot):
            base = static_slot * rows * LANE_TILES
            for s in range(LANE_TILES):
                out_ref[:, s * 128:(s + 1) * 128] = buf[pl.ds(base + s, rows, stride=LANE_TILES), :].astype(BF)


def _gather_call(src, idx):
    rows = 256
    nblk = idx.shape[0] // rows
    idx3 = idx.reshape(nblk, 1, rows)
    return pl.pallas_call(
        functools.partial(_gather_kernel, rows=rows, nblk=nblk),
        grid=(nblk,),
        in_specs=[
            pl.BlockSpec((None, 1, rows), lambda i: (i, 0, 0), memory_space=pltpu.SMEM),
            pl.BlockSpec((None, 1, rows), lambda i: (jnp.minimum(i + 1, nblk - 1), 0, 0),
                         memory_space=pltpu.SMEM),
            pl.BlockSpec(memory_space=pl.ANY),
        ],
        out_specs=pl.BlockSpec((rows, D_MODEL), lambda i: (i, 0)),
        out_shape=jax.ShapeDtypeStruct((idx.shape[0], D_MODEL), BF),
        scratch_shapes=[pltpu.VMEM((2 * rows * LANE_TILES, 128), F32), pltpu.SemaphoreType.DMA((2,))],
        compiler_params=_params(("arbitrary",), 32),
        name="moe_gather",
    )(idx3, idx3, src)


def _dispatch_plan(te):
    flat_e = te[:, :TOP_K].reshape(-1)
    experts = jnp.arange(N_EXPERTS, dtype=jnp.int32)
    onehot = (flat_e[:, None] == experts[None, :]).astype(jnp.int32)
    csum = jnp.cumsum(onehot, axis=0)
    rank = jnp.sum(csum * onehot, axis=1) - 1
    counts = csum[-1]
    padded = (counts + MOE_BLK - 1) // MOE_BLK * MOE_BLK
    pad_end = jnp.cumsum(padded)
    pad_start = pad_end - padded
    dest = (jnp.sum(pad_start[None, :] * onehot, axis=1) + rank).astype(jnp.int32)
    tok_of_row = jnp.zeros((MOE_ROWS,), jnp.int32).at[dest].set(
        jnp.arange(N_ASSIGN, dtype=jnp.int32) // TOP_K)
    dest2 = dest.reshape(M_ROWS, TOP_K)

    valid_end = pad_start + counts

    later_used = jnp.logical_and(experts[None, :] > experts[:, None], (padded > 0)[None, :])
    nxt_of_e = jnp.min(jnp.where(later_used, experts[None, :], N_EXPERTS), axis=1)
    nxt_of_e = jnp.where(nxt_of_e == N_EXPERTS, -1, nxt_of_e)

    def plan(blk):
        ids = jnp.arange(MOE_ROWS // blk, dtype=jnp.int32)
        starts = ids * blk
        in_range = jnp.minimum(starts, pad_end[-1] - blk)
        block_e = jnp.sum((pad_end[None, :] <= in_range[:, None]).astype(jnp.int32), axis=1)
        onehot_e = (block_e[:, None] == experts[None, :]).astype(jnp.int32)
        valid = jnp.clip(jnp.sum(valid_end[None, :] * onehot_e, axis=1) - starts, 0, blk)
        live = (valid > 0).astype(jnp.int32)
        src = lax.cummax(jnp.where(live == 1, ids, 0), axis=0)
        nxt = jnp.sum(nxt_of_e[None, :] * onehot_e, axis=1)
        return block_e.astype(jnp.int32), src.astype(jnp.int32), live, nxt.astype(jnp.int32)

    return tok_of_row, dest2[:, 0], dest2[:, 1], plan


def kernel(x_prompt, x_sample, state_conv, state_pool, c_prompt, c_sample, ln_in_g, ln_in_b, w_ada, b_ada,
           w_mix_in, sgu_g, sgu_b, w_spatial, b_spatial, conv_w, conv_b, w_pool, pool_scale, w_branch, w_o,
           ln_g, ln_b, w_ffn_gate, w_ffn_up, w_ffn_down, w_router, w_exp_gate, w_exp_up, w_exp_down):
    c_all = jnp.concatenate(
        [c_sample, c_prompt, jnp.zeros((C_ROWS - BATCH - DEC_BATCH, D_MODEL), F32)], axis=0)
    mods = _ada_call(c_all, w_ada, b_ada)

    wsp_bf = w_spatial.astype(BF)
    bsp_full = jnp.repeat(b_spatial.transpose(0, 2, 1), SGU_HD, axis=-1)
    wv = jnp.repeat(w_spatial[:, :, :DEC_SEQ, :DEC_SEQ].transpose(0, 2, 3, 1), SGU_HD, axis=-1)
    bv = bsp_full[:, :DEC_SEQ]
    wpool_bf = w_pool.astype(BF)
    state_conv_t = state_conv.transpose(0, 2, 1, 3)
    state_pool_t = state_pool.transpose(0, 2, 1, 3)
    wr_pad = jnp.pad(w_router, ((0, 0), (0, 0), (0, ROUTER_LANES - N_EXPERTS)))

    x, h = _ln_in_call(x_prompt.reshape(P_ROWS, D_MODEL), x_sample.transpose(1, 0, 2).reshape(S_ROWS, D_MODEL),
                       ln_in_g, ln_in_b, mods)

    conv_p, conv_s, pool_p, pool_s, v_s = [], [], [], [], []
    for l in range(DEPTH):
        i = l // 2
        proj = _dense_mm(h, [w_mix_in], l, bm=1024, tn=1024, vmem_mib=48, name="mix_in")
        br, cp, pp = _mid_prompt_call(proj, l, sgu_g, sgu_b, wsp_bf, bsp_full, conv_w, conv_b, wpool_bf,
                                      pool_scale)
        br3, vs, cs, ps = _mid_sample_call(
            proj.reshape(M_ROWS // SLAB, SLAB, MIX_IN_COLS), br.reshape(M_ROWS // SLAB, SLAB, N_BRANCH * MIX_W),
            l, state_conv_t, state_pool_t, sgu_g, sgu_b, wv, bv, conv_w, conv_b, wpool_bf, pool_scale)
        conv_p.append(cp[:, 8 - (CONV_W - 1):, :])
        pool_p.append(pp[:, 16 - POOL_BUF:, :])
        conv_s.append(cs.transpose(1, 0, 2))
        pool_s.append(ps.transpose(1, 0, 2))
        v_s.append(vs.transpose(1, 0, 2))
        merged = _merge_call(br3.reshape(M_ROWS, N_BRANCH * MIX_W), proj, w_branch, l)
        next_mod = (l + 1, 0) if l + 1 < DEPTH else None
        if l % 2 == 0:
            x, h2 = _post_call(merged, x, mods, l, 0, ln_g, ln_b, (l, 1), proj=(w_o, l), name="out_proj_post")
            hid = _dense_mm(h2, [w_ffn_gate, w_ffn_up], i, bm=1024, tn=512, swiglu=True, out_dtype=BF,
                            vmem_mib=48, name="ffn_up")
            f = _dense_mm(hid, [w_ffn_down], i, bm=512, tn=512, vmem_mib=52, name="ffn_down")
            outs = _post_call(f, x, mods, l, 1, ln_g, ln_b, next_mod, name="post_ffn")
        else:
            x, h2, te, tw = _post_call(merged, x, mods, l, 0, ln_g, ln_b, (l, 1), proj=(w_o, l),
                                       h_token_major=True, wr=wr_pad, wr_layer=i, name="out_proj_post_router")
            tok_of_row, dest0, dest1, plan = _dispatch_plan(te)
            rows = _gather_call(h2.reshape(M_ROWS, LANE_TILES, 128), tok_of_row)
            moe_plan = plan(MOE_BLK)
            hid = _gmm_call(rows, [w_exp_gate, w_exp_up], i, moe_plan, bm=MOE_BLK, tn=1024, swiglu=True,
                            out_dtype=BF, vmem_mib=48, name="moe_up")
            y = _gmm_call(hid, [w_exp_down], i, moe_plan, bm=MOE_BLK, tn=512, swiglu=False,
                          out_dtype=F32, vmem_mib=56, name="moe_down")
            outs = _post_call(y, x, mods, l, 1, ln_g, ln_b, next_mod, gather=(dest0, dest1, tw),
                              name="moe_combine")
        if next_mod is None:
            x_prompt_out, x_sample_out = outs
        else:
            x, h = outs

    y_prompt = x_prompt_out.reshape(BATCH, SEQ, D_MODEL)
    y_sample = x_sample_out.reshape(DEC_SEQ, DEC_BATCH, D_MODEL).transpose(1, 0, 2)
    return (y_prompt, y_sample, jnp.stack(conv_p, 0), jnp.stack(conv_s, 0), jnp.stack(pool_p, 0),
            jnp.stack(pool_s, 0), jnp.stack(v_s, 0))
```

```python
import functools

import jax
import jax.numpy as jnp
from jax import lax
from jax.experimental import pallas as pl
from jax.experimental.pallas import tpu as pltpu

D_MODEL = 2048
BATCH = 4
SEQ = 2048
DEPTH = 4
DEC_BATCH = 128
DEC_SEQ = 8
PAST_LEN = 16384
MIX_W = 1024
CHUNK = 128
SGU_HEADS = 8
SGU_HD = MIX_W // SGU_HEADS
CONV_W = 3
POOL_WINDOWS = (2, 4, 8, 16)
POOL_GROUPS = 4
POOL_GD = MIX_W // POOL_GROUPS
POOL_BUF = 15
N_BRANCH = 3
MIX_IN_COLS = 6 * MIX_W + N_BRANCH * D_MODEL
D_FF = 5632
N_EXPERTS = 8
TOP_K = 2
E_FF = 7168
DEEPNORM_ALPHA = (2 * DEPTH) ** 0.25
LN_EPS = 1e-5

P_ROWS = BATCH * SEQ
S_ROWS = DEC_SEQ * DEC_BATCH
M_ROWS = P_ROWS + S_ROWS
SLAB = DEC_BATCH
C_ROWS = 136
ROUTER_LANES = 128
MOE_BLK = 512
N_ASSIGN = M_ROWS * TOP_K
MOE_ROWS = N_ASSIGN + N_EXPERTS * MOE_BLK
GATHER_ROWS = 256
POST_ROWS = 256
LANE_TILES = D_MODEL // 128

BF = jnp.bfloat16
F32 = jnp.float32

_MIB = 1 << 20


def _params(semantics, vmem_mib):
    return pltpu.CompilerParams(dimension_semantics=semantics, vmem_limit_bytes=vmem_mib * _MIB)


def _dot(a, b):
    return jnp.dot(a, b, preferred_element_type=F32)


def _ln(x, g, b):
    mu = jnp.mean(x, axis=-1, keepdims=True)
    xc = x - mu
    var = jnp.mean(xc * xc, axis=-1, keepdims=True)
    return xc * lax.rsqrt(var + LN_EPS) * g + b


def _slab_bcast(fn, x, *slabs):
    rows, d = x.shape
    x3 = x.reshape(rows // SLAB, SLAB, d)
    return fn(x3, *[s[None] for s in slabs]).reshape(rows, d)


def _modulate(x, shift, scale):
    return _slab_bcast(lambda a, sh, sc: a * (1.0 + sc) + sh, x, shift, scale)


def _mod_spec(l, n, k, tm):
    return pl.BlockSpec((None, None, None, None, SLAB, D_MODEL),
                        lambda i: (l, n, k, jnp.minimum(i * tm // SEQ, BATCH), 0, 0))


def _ada_kernel(c_ref, w_ref, b_ref, o_ref):
    c = c_ref[...]
    s = (c * jax.nn.sigmoid(c)).astype(BF)
    ada = _dot(s, w_ref[...].astype(BF)) + b_ref[...]
    for g in range(BATCH):
        o_ref[g] = jnp.broadcast_to(ada[DEC_BATCH + g:DEC_BATCH + g + 1, :], o_ref.shape[1:])
    o_ref[BATCH] = ada[0:DEC_BATCH, :]


def _ada_call(c_all, w_ada, b_ada):
    tn = 1024
    per_part = D_MODEL // tn
    return pl.pallas_call(
        _ada_kernel,
        grid=(DEPTH, 2, 3 * per_part),
        in_specs=[
            pl.BlockSpec((C_ROWS, D_MODEL), lambda l, n, j: (0, 0)),
            pl.BlockSpec((None, None, D_MODEL, tn), lambda l, n, j: (l, n, 0, j)),
            pl.BlockSpec((None, None, 1, tn), lambda l, n, j: (l, n, 0, j)),
        ],
        out_specs=pl.BlockSpec((None, None, None, BATCH + 1, SLAB, tn),
                               lambda l, n, j: (l, n, j // per_part, 0, 0, j % per_part)),
        out_shape=jax.ShapeDtypeStruct((DEPTH, 2, 3, BATCH + 1, SLAB, D_MODEL), F32),
        compiler_params=_params(("arbitrary",) * 3, 40),
        name="ada",
    )(c_all, w_ada, b_ada.reshape(DEPTH, 2, 1, 3 * D_MODEL))


def _ln_in_kernel(xp_ref, xs_ref, g_ref, b_ref, sh_ref, sc_ref, xo_ref, ho_ref, *, prompt_tiles):
    def emit(x_ref):
        xn = _ln(x_ref[...], g_ref[...], b_ref[...])
        xo_ref[...] = xn
        ho_ref[...] = _modulate(xn, sh_ref[...], sc_ref[...]).astype(BF)

    i = pl.program_id(0)
    pl.when(i < prompt_tiles)(lambda: emit(xp_ref))
    pl.when(i >= prompt_tiles)(lambda: emit(xs_ref))


def _ln_in_call(x_prompt_rows, x_sample_rows, g, b, mods):
    tm = POST_ROWS
    prompt_tiles = P_ROWS // tm
    row = pl.BlockSpec((tm, D_MODEL), lambda i: (i, 0))
    vec = pl.BlockSpec((1, D_MODEL), lambda i: (0, 0))
    return pl.pallas_call(
        functools.partial(_ln_in_kernel, prompt_tiles=prompt_tiles),
        grid=(M_ROWS // tm,),
        in_specs=[
            pl.BlockSpec((tm, D_MODEL), lambda i: (jnp.minimum(i, prompt_tiles - 1), 0)),
            pl.BlockSpec((tm, D_MODEL), lambda i: (jnp.maximum(i - prompt_tiles, 0), 0)),
            vec, vec, _mod_spec(0, 0, 0, tm), _mod_spec(0, 0, 1, tm)],
        out_specs=[row, row],
        out_shape=[jax.ShapeDtypeStruct((M_ROWS, D_MODEL), F32), jax.ShapeDtypeStruct((M_ROWS, D_MODEL), BF)],
        compiler_params=_params(("arbitrary",), 40),
        name="ln_in",
    )(x_prompt_rows, x_sample_rows, g.reshape(1, D_MODEL), b.reshape(1, D_MODEL), mods, mods)


def _gmm_kernel(be_ref, src_ref, live_ref, nxt_ref, lhs_ref, *refs, n_w, swiglu, w_layer, tn, nj):
    del src_ref
    w_hbm = refs[:n_w]
    o_ref = refs[n_w]
    stage = refs[n_w + 1:2 * n_w + 1]
    wb_refs = refs[2 * n_w + 1:3 * n_w + 1]
    sem = refs[3 * n_w + 1]
    j = pl.program_id(0)
    b = pl.program_id(1)

    def tile_copy(k, e, jj):
        col = pl.multiple_of(jj * tn, tn)
        return pltpu.make_async_copy(w_hbm[k].at[w_layer, e, :, pl.ds(col, tn)], stage[k], sem.at[k])

    def start_tiles(e, jj):
        for k in range(n_w):
            tile_copy(k, e, jj).start()

    e_here = be_ref[b]
    changed = jnp.logical_or(b == 0, e_here != be_ref[jnp.maximum(b - 1, 0)])

    @pl.when(jnp.logical_and(j == 0, b == 0))
    def _():
        start_tiles(e_here, 0)

    @pl.when(changed)
    def _():
        for k in range(n_w):
            tile_copy(k, e_here, j).wait()
            wb_refs[k][...] = stage[k][...].astype(BF)
        nxt = nxt_ref[b]

        @pl.when(nxt >= 0)
        def _():
            start_tiles(nxt, j)

        @pl.when(jnp.logical_and(nxt < 0, j + 1 < nj))
        def _():
            start_tiles(be_ref[0], j + 1)

    live = live_ref[b] == 1

    @pl.when(live)
    def _():
        lhs = lhs_ref[...]
        if swiglu:
            a = _dot(lhs, wb_refs[0][...])
            u = _dot(lhs, wb_refs[1][...])
            o_ref[...] = (a * jax.nn.sigmoid(a) * u).astype(o_ref.dtype)
        else:
            o_ref[...] = _dot(lhs, wb_refs[0][...]).astype(o_ref.dtype)

    @pl.when(jnp.logical_not(live))
    def _():
        o_ref[...] = jnp.zeros_like(o_ref)


def _gmm_call(lhs, weights, w_layer, plan, *, bm, tn, swiglu, out_dtype, vmem_mib, name):
    rows, k = lhs.shape
    n = weights[0].shape[-1]
    n_w = len(weights)
    nj = n // tn
    grid_spec = pltpu.PrefetchScalarGridSpec(
        num_scalar_prefetch=4,
        grid=(nj, rows // bm),
        in_specs=[pl.BlockSpec((bm, k), lambda j, b, be, sb, lv, nx: (sb[b], 0))]
        + [pl.BlockSpec(memory_space=pl.ANY)] * n_w,
        out_specs=pl.BlockSpec((bm, tn), lambda j, b, be, sb, lv, nx: (b, j)),
        scratch_shapes=[pltpu.VMEM((k, tn), F32)] * n_w + [pltpu.VMEM((k, tn), BF)] * n_w
        + [pltpu.SemaphoreType.DMA((n_w,))],
    )
    return pl.pallas_call(
        functools.partial(_gmm_kernel, n_w=n_w, swiglu=swiglu, w_layer=w_layer, tn=tn, nj=nj),
        grid_spec=grid_spec,
        out_shape=jax.ShapeDtypeStruct((rows, n), out_dtype),
        compiler_params=_params(("arbitrary", "arbitrary"), vmem_mib),
        name=name,
    )(*plan, lhs, *weights)


def _dense_mm(lhs, w, w_layer, *, bm, tn, swiglu=False, out_dtype=F32, vmem_mib, name):
    nb = lhs.shape[0] // bm
    plan = (jnp.zeros((nb,), jnp.int32), jnp.arange(nb, dtype=jnp.int32), jnp.ones((nb,), jnp.int32),
            jnp.full((nb,), -1, jnp.int32))
    weights = [wi.reshape(wi.shape[0], 1, *wi.shape[1:]) for wi in w]
    return _gmm_call(lhs, weights, w_layer, plan, bm=bm, tn=tn, swiglu=swiglu, out_dtype=out_dtype,
                     vmem_mib=vmem_mib, name=name)


def _mid_prompt_kernel(*refs, rows, tiles_per_seq):
    br_ref = refs[14]
    i = pl.program_id(0)

    @pl.when(i < P_ROWS // rows)
    def _():
        _mid_prompt_tile(*refs, rows=rows, tile_in_seq=i % tiles_per_seq)

    @pl.when(i >= P_ROWS // rows)
    def _():
        br_ref[...] = jnp.zeros_like(br_ref)


def _mid_prompt_tile(u_ref, v_ref, gb_ref, gc_ref, hb_ref, p_ref, sg_ref, sb_ref, wsp_ref, bsp_ref,
                     cw_ref, cb_ref, wpool_ref, pscale_ref,
                     br_ref, convo_ref, poolo_ref, zh_ref, ph_ref, *, rows, tile_in_seq):

    @pl.when(tile_in_seq == 0)
    def _():
        zh_ref[...] = jnp.zeros_like(zh_ref)
        ph_ref[...] = jnp.zeros_like(ph_ref)

    v = _ln(v_ref[...], sg_ref[...], sb_ref[...])
    vb = v.astype(BF)
    r_id = lax.broadcasted_iota(jnp.int32, (CHUNK, CHUNK), 0)
    c_id = lax.broadcasted_iota(jnp.int32, (CHUNK, CHUNK), 1)
    causal = c_id <= r_id
    w_heads = [jnp.where(causal, wsp_ref[h], jnp.zeros((CHUNK, CHUNK), BF)) for h in range(SGU_HEADS)]
    chunks = []
    for c in range(rows // CHUNK):
        heads = [
            _dot(w_heads[h], vb[c * CHUNK:(c + 1) * CHUNK, h * SGU_HD:(h + 1) * SGU_HD])
            for h in range(SGU_HEADS)
        ]
        chunks.append(jnp.concatenate(heads, axis=1) + bsp_ref[...])
    s = jnp.concatenate(chunks, axis=0)
    br_ref[:, 0:MIX_W] = (u_ref[...] * s).astype(BF)

    z = gc_ref[...] * hb_ref[...]
    ze = jnp.concatenate([zh_ref[...], z], axis=0)
    z1 = pltpu.roll(ze, 1, 0)[8:]
    z2 = pltpu.roll(ze, 2, 0)[8:]
    conv = cb_ref[...] + z2 * cw_ref[0:1, :]
    conv = conv + z1 * cw_ref[1:2, :]
    conv = conv + z * cw_ref[2:3, :]
    br_ref[:, MIX_W:2 * MIX_W] = (gb_ref[...] * conv).astype(BF)

    p = p_ref[...]
    pe = jnp.concatenate([ph_ref[...], p], axis=0)
    win = pe + pltpu.roll(pe, 1, 0)
    sums = [win[16:, 0:POOL_GD]]
    shift = 2
    for j in range(1, POOL_GROUPS):
        win = win[:, POOL_GD:]
        win = win + pltpu.roll(win, shift, 0)
        sums.append(win[16:, 0:POOL_GD])
        shift *= 2
    pos1 = (tile_in_seq * rows + 1 + lax.broadcasted_iota(jnp.int32, (rows, 1), 0)).astype(F32)
    mixed = []
    for j, w in enumerate(POOL_WINDOWS):
        mean = sums[j] / jnp.minimum(float(w), pos1)
        pooled = mean - p[:, j * POOL_GD:(j + 1) * POOL_GD]
        mixed.append(_dot(pooled.astype(BF), wpool_ref[j]))
    y_c = jnp.concatenate(mixed, axis=1) * pscale_ref[...]
    br_ref[:, 2 * MIX_W:3 * MIX_W] = y_c.astype(BF)

    z_tail = z[rows - 8:, :]
    p_tail = p[rows - 16:, :]
    zh_ref[...] = z_tail
    ph_ref[...] = p_tail
    convo_ref[...] = z_tail
    poolo_ref[...] = p_tail


def _mid_prompt_call(proj, l, sgu_g, sgu_b, wsp_bf, bsp_full, conv_w, conv_b, wpool_bf, pool_scale):
    rows = 256
    tiles_per_seq = SEQ // rows

    def col(c):
        return pl.BlockSpec((rows, MIX_W), lambda i, c=c: (i, c))

    vec = pl.BlockSpec((None, 1, MIX_W), lambda i: (l, 0, 0))
    in_specs = [col(c) for c in range(6)] + [
        vec, vec,
        pl.BlockSpec((None, SGU_HEADS, CHUNK, CHUNK), lambda i: (l, 0, 0, 0)),
        pl.BlockSpec((None, CHUNK, MIX_W), lambda i: (l, 0, 0)),
        pl.BlockSpec((None, CONV_W, MIX_W), lambda i: (l, 0, 0)),
        vec,
        pl.BlockSpec((None, POOL_GROUPS, POOL_GD, POOL_GD), lambda i: (l, 0, 0, 0)),
        vec,
    ]
    def seq_of(i):
        return jnp.minimum(i // tiles_per_seq, BATCH - 1)

    out_specs = [
        pl.BlockSpec((rows, N_BRANCH * MIX_W), lambda i: (i, 0)),
        pl.BlockSpec((None, 8, MIX_W), lambda i: (seq_of(i), 0, 0)),
        pl.BlockSpec((None, 16, MIX_W), lambda i: (seq_of(i), 0, 0)),
    ]
    out_shape = [
        jax.ShapeDtypeStruct((M_ROWS, N_BRANCH * MIX_W), BF),
        jax.ShapeDtypeStruct((BATCH, 8, MIX_W), F32),
        jax.ShapeDtypeStruct((BATCH, 16, MIX_W), F32),
    ]
    return pl.pallas_call(
        functools.partial(_mid_prompt_kernel, rows=rows, tiles_per_seq=tiles_per_seq),
        grid=(M_ROWS // rows,),
        in_specs=in_specs,
        out_specs=out_specs,
        out_shape=out_shape,
        scratch_shapes=[pltpu.VMEM((8, MIX_W), F32), pltpu.VMEM((16, MIX_W), F32)],
        compiler_params=_params(("arbitrary",), 48),
        name="mid_prompt",
    )(proj, proj, proj, proj, proj, proj, sgu_g.reshape(DEPTH, 1, MIX_W), sgu_b.reshape(DEPTH, 1, MIX_W),
      wsp_bf, bsp_full, conv_w, conv_b.reshape(DEPTH, 1, MIX_W), wpool_bf, pool_scale.reshape(DEPTH, 1, MIX_W))


def _mid_sample_kernel(u_ref, v_ref, gb_ref, gc_ref, hb_ref, p_ref, cs_ref, ps_ref, sg_ref, sb_ref, wv_ref,
                       bv_ref, cw_ref, cb_ref, wpool_ref, pscale_ref, br_in_ref,
                       br_ref, vo_ref, convo_ref, poolo_ref, *, bs):
    del br_in_ref
    sg = sg_ref[...]
    sb = sb_ref[...]
    v = [_ln(v_ref[t], sg, sb) for t in range(DEC_SEQ)]
    for t in range(DEC_SEQ):
        vo_ref[t] = v[t]
        acc = v[0] * wv_ref[t, 0:1, :]
        for s in range(1, t + 1):
            acc = acc + v[s] * wv_ref[t, s:s + 1, :]
        br_ref[t, :, 0:MIX_W] = (u_ref[t] * (acc + bv_ref[t:t + 1, :])).astype(BF)

    zc = [cs_ref[0], cs_ref[1]] + [gc_ref[t] * hb_ref[t] for t in range(DEC_SEQ)]
    for t in range(DEC_SEQ):
        conv = cb_ref[...] + zc[t] * cw_ref[0:1, :]
        conv = conv + zc[t + 1] * cw_ref[1:2, :]
        conv = conv + zc[t + 2] * cw_ref[2:3, :]
        br_ref[t, :, MIX_W:2 * MIX_W] = (gb_ref[t] * conv).astype(BF)
    convo_ref[0] = zc[DEC_SEQ]
    convo_ref[1] = zc[DEC_SEQ + 1]

    pc = [ps_ref[i] for i in range(POOL_BUF)] + [p_ref[t] for t in range(DEC_SEQ)]
    for i in range(POOL_BUF):
        poolo_ref[i] = pc[DEC_SEQ + i]
    for j, w in enumerate(POOL_WINDOWS):
        lanes = slice(j * POOL_GD, (j + 1) * POOL_GD)
        pooled = []
        for t in range(DEC_SEQ):
            end = POOL_BUF + t
            tot = pc[end][:, lanes]
            for k in range(1, w):
                tot = tot + pc[end - k][:, lanes]
            cnt = float(min(w, PAST_LEN + t + 1))
            pooled.append(tot / cnt - pc[end][:, lanes])
        pooled = jnp.concatenate(pooled, axis=0).astype(BF)
        mixed = _dot(pooled, wpool_ref[j]) * pscale_ref[:, lanes]
        for t in range(DEC_SEQ):
            br_ref[t, :, 2 * MIX_W + j * POOL_GD:2 * MIX_W + (j + 1) * POOL_GD] = (
                mixed[t * bs:(t + 1) * bs].astype(BF))


def _mid_sample_call(proj3, br3, l, state_conv_t, state_pool_t, sgu_g, sgu_b, wv, bv, conv_w, conv_b, wpool_bf,
                     pool_scale):
    bs = 32
    t_block = P_ROWS // SLAB // DEC_SEQ
    n_slabs = M_ROWS // SLAB

    def col(c):
        return pl.BlockSpec((DEC_SEQ, bs, MIX_W), lambda i, c=c: (t_block, i, c))

    vec = pl.BlockSpec((None, 1, MIX_W), lambda i: (l, 0, 0))
    in_specs = [col(c) for c in range(6)] + [
        pl.BlockSpec((None, CONV_W - 1, bs, MIX_W), lambda i: (l, 0, i, 0)),
        pl.BlockSpec((None, POOL_BUF, bs, MIX_W), lambda i: (l, 0, i, 0)),
        vec, vec,
        pl.BlockSpec((None, DEC_SEQ, DEC_SEQ, MIX_W), lambda i: (l, 0, 0, 0)),
        pl.BlockSpec((None, DEC_SEQ, MIX_W), lambda i: (l, 0, 0)),
        pl.BlockSpec((None, CONV_W, MIX_W), lambda i: (l, 0, 0)),
        vec,
        pl.BlockSpec((None, POOL_GROUPS, POOL_GD, POOL_GD), lambda i: (l, 0, 0, 0)),
        vec,
        pl.BlockSpec(memory_space=pl.ANY),
    ]
    out_specs = [
        pl.BlockSpec((DEC_SEQ, bs, N_BRANCH * MIX_W), lambda i: (t_block, i, 0)),
        pl.BlockSpec((DEC_SEQ, bs, MIX_W), lambda i: (0, i, 0)),
        pl.BlockSpec((CONV_W - 1, bs, MIX_W), lambda i: (0, i, 0)),
        pl.BlockSpec((POOL_BUF, bs, MIX_W), lambda i: (0, i, 0)),
    ]
    out_shape = [
        jax.ShapeDtypeStruct((n_slabs, SLAB, N_BRANCH * MIX_W), BF),
        jax.ShapeDtypeStruct((DEC_SEQ, DEC_BATCH, MIX_W), F32),
        jax.ShapeDtypeStruct((CONV_W - 1, DEC_BATCH, MIX_W), F32),
        jax.ShapeDtypeStruct((POOL_BUF, DEC_BATCH, MIX_W), F32),
    ]
    return pl.pallas_call(
        functools.partial(_mid_sample_kernel, bs=bs),
        grid=(DEC_BATCH // bs,),
        in_specs=in_specs,
        out_specs=out_specs,
        out_shape=out_shape,
        input_output_aliases={len(in_specs) - 1: 0},
        compiler_params=_params(("arbitrary",), 48),
        name="mid_sample",
    )(proj3, proj3, proj3, proj3, proj3, proj3, state_conv_t, state_pool_t,
      sgu_g.reshape(DEPTH, 1, MIX_W), sgu_b.reshape(DEPTH, 1, MIX_W), wv, bv, conv_w,
      conv_b.reshape(DEPTH, 1, MIX_W), wpool_bf, pool_scale.reshape(DEPTH, 1, MIX_W), br3)


def _merge_kernel(br_ref, w_ref, g0_ref, g1_ref, g2_ref, o_ref, wb_ref):
    @pl.when(pl.program_id(1) == 0)
    def _():
        wb_ref[...] = w_ref[...].astype(BF)

    acc = None
    for r, g_ref in enumerate((g0_ref, g1_ref, g2_ref)):
        term = jax.nn.sigmoid(g_ref[...]) * _dot(br_ref[:, r * MIX_W:(r + 1) * MIX_W], wb_ref[r])
        acc = term if acc is None else acc + term
    o_ref[...] = acc.astype(BF)


def _merge_call(br, proj, w_branch, l):
    tm, tn = 1024, 512
    gate_col0 = 6 * MIX_W // tn
    per_gate = D_MODEL // tn

    def gate(r):
        return pl.BlockSpec((tm, tn), lambda j, i, r=r: (i, gate_col0 + r * per_gate + j))

    return pl.pallas_call(
        _merge_kernel,
        grid=(D_MODEL // tn, M_ROWS // tm),
        in_specs=[
            pl.BlockSpec((tm, N_BRANCH * MIX_W), lambda j, i: (i, 0)),
            pl.BlockSpec((None, N_BRANCH, MIX_W, tn), lambda j, i: (l, 0, 0, j)),
            gate(0), gate(1), gate(2),
        ],
        out_specs=pl.BlockSpec((tm, tn), lambda j, i: (i, j)),
        out_shape=jax.ShapeDtypeStruct((M_ROWS, D_MODEL), BF),
        scratch_shapes=[pltpu.VMEM((N_BRANCH, MIX_W, tn), BF)],
        compiler_params=_params(("arbitrary", "arbitrary"), 56),
        name="merge",
    )(br, w_branch, proj, proj, proj)


def _router_top2(h, wr_ref, te_ref, tw_ref):
    rows = h.shape[0]
    logits = jnp.dot(h, wr_ref[...], preferred_element_type=F32, precision=lax.Precision.HIGHEST)
    lane = lax.broadcasted_iota(jnp.int32, (rows, ROUTER_LANES), 1)
    lane_f = lane.astype(F32)
    valid = lane < N_EXPERTS
    lg = jnp.where(valid, logits, -jnp.inf)
    ex = jnp.exp(lg - jnp.max(lg, axis=-1, keepdims=True))
    probs = ex / jnp.sum(ex, axis=-1, keepdims=True)
    cand = jnp.where(valid, probs, -1.0)
    p1 = jnp.max(cand, axis=-1, keepdims=True)
    i1 = jnp.min(jnp.where(cand == p1, lane_f, float(ROUTER_LANES)), axis=-1, keepdims=True)
    cand = jnp.where(lane_f == i1, -1.0, cand)
    p2 = jnp.max(cand, axis=-1, keepdims=True)
    i2 = jnp.min(jnp.where(cand == p2, lane_f, float(ROUTER_LANES)), axis=-1, keepdims=True)
    tot = p1 + p2
    te_ref[...] = jnp.where(lane == 0, i1, jnp.where(lane == 1, i2, 0.0)).astype(jnp.int32)
    tw_ref[...] = jnp.where(lane == 0, p1 / tot, jnp.where(lane == 1, p2 / tot, 0.0))


def _post_kernel(*refs, rows, nblk, gathered, projected, emit_h, router, h_token_major):
    it = iter(refs)
    if gathered:
        d0c, d1c, d0n, d1n, y_hbm, tw_in_ref = (next(it) for _ in range(6))
    else:
        f_ref = next(it)
    if projected:
        w_ref = next(it)
    x_ref, gate_ref, lng_ref, lnb_ref = (next(it) for _ in range(4))
    sh_ref = sc_ref = wr_ref = ho_ref = te_ref = tw_ref = None
    if emit_h:
        sh_ref, sc_ref = next(it), next(it)
    if router:
        wr_ref = next(it)
    xo_ref = next(it)
    xo_sample_ref = None if emit_h else next(it)
    if emit_h:
        ho_ref = next(it)
    if router:
        te_ref, tw_ref = next(it), next(it)

    i = pl.program_id(0)
    if projected:
        wb_ref = next(it)

        @pl.when(i == 0)
        def _():
            wb_ref[...] = w_ref[...].astype(BF)

    if gathered:
        buf, sem = next(it), next(it)
        slot = i % 2

        def issue(d0_ref, d1_ref, s):
            def body(r, carry):
                pltpu.make_async_copy(y_hbm.at[pl.ds(d0_ref[0, r], 1)], buf.at[s, 0, pl.ds(r, 1)], sem.at[s]).start()
                pltpu.make_async_copy(y_hbm.at[pl.ds(d1_ref[0, r], 1)], buf.at[s, 1, pl.ds(r, 1)], sem.at[s]).start()
                return carry

            lax.fori_loop(0, rows, body, 0)

        @pl.when(i == 0)
        def _():
            issue(d0c, d1c, 0)

        @pl.when(i + 1 < nblk)
        def _():
            issue(d0n, d1n, 1 - slot)

        def wait(r, carry):
            pltpu.make_async_copy(y_hbm.at[pl.ds(0, 1)], buf.at[slot, 0, pl.ds(r, 1)], sem.at[slot]).wait()
            pltpu.make_async_copy(y_hbm.at[pl.ds(0, 1)], buf.at[slot, 1, pl.ds(r, 1)], sem.at[slot]).wait()
            return carry

        lax.fori_loop(0, rows, wait, 0)
        f = buf[slot, 0] * tw_in_ref[:, 0:1] + buf[slot, 1] * tw_in_ref[:, 1:2]
    elif projected:
        f = _dot(f_ref[...], wb_ref[...])
    else:
        f = f_ref[...]

    y = _slab_bcast(lambda a, g: a * g, f, gate_ref[...]) + DEEPNORM_ALPHA * x_ref[...]
    xn = _ln(y, lng_ref[...], lnb_ref[...])
    if emit_h:
        xo_ref[...] = xn
    else:
        @pl.when(i < P_ROWS // rows)
        def _():
            xo_ref[...] = xn

        @pl.when(i >= P_ROWS // rows)
        def _():
            xo_sample_ref[...] = xn

    if emit_h:
        h = _modulate(xn, sh_ref[...], sc_ref[...])
        if h_token_major:
            for s in range(LANE_TILES):
                ho_ref[pl.ds(s, rows, stride=LANE_TILES), :] = h[:, s * 128:(s + 1) * 128]
        else:
            ho_ref[...] = h.astype(ho_ref.dtype)
        if router:
            _router_top2(h, wr_ref, te_ref, tw_ref)


def _post_call(src, x, mods, l, n, ln_g, ln_b, next_mod, *, h_token_major=False, wr=None, wr_layer=0,
               gather=None, proj=None, name):
    rows = POST_ROWS
    nblk = M_ROWS // rows
    emit_h = next_mod is not None
    router = wr is not None
    gathered = gather is not None
    projected = proj is not None
    row = pl.BlockSpec((rows, D_MODEL), lambda i: (i, 0))
    vec = pl.BlockSpec((None, None, 1, D_MODEL), lambda i: (l, n, 0, 0))
    in_specs, args, scratch = [], [], []
    if projected:
        w, w_layer = proj
        in_specs += [row, pl.BlockSpec((None, D_MODEL, D_MODEL), lambda i: (w_layer, 0, 0),
                                       pipeline_mode=pl.Buffered(1))]
        args += [src, w]
        scratch = [pltpu.VMEM((D_MODEL, D_MODEL), BF)]
    if gathered:
        dest0, dest1, tw = gather
        cur = pl.BlockSpec((None, 1, rows), lambda i: (i, 0, 0), memory_space=pltpu.SMEM)
        nxt = pl.BlockSpec((None, 1, rows), lambda i: (jnp.minimum(i + 1, nblk - 1), 0, 0),
                           memory_space=pltpu.SMEM)
        d0, d1 = dest0.reshape(nblk, 1, rows), dest1.reshape(nblk, 1, rows)
        in_specs += [cur, cur, nxt, nxt, pl.BlockSpec(memory_space=pl.ANY),
                     pl.BlockSpec((rows, ROUTER_LANES), lambda i: (i, 0))]
        args += [d0, d1, d0, d1, src, tw]
        scratch = [pltpu.VMEM((2, TOP_K, rows, D_MODEL), F32), pltpu.SemaphoreType.DMA((2,))]
    elif not projected:
        in_specs.append(row)
        args.append(src)
    in_specs += [row, _mod_spec(l, n, 2, rows), vec, vec]
    args += [x, mods, ln_g.reshape(DEPTH, 2, 1, D_MODEL), ln_b.reshape(DEPTH, 2, 1, D_MODEL)]
    if emit_h:
        out_specs = [row]
        out_shape = [jax.ShapeDtypeStruct((M_ROWS, D_MODEL), F32)]
    else:
        prompt_tiles = P_ROWS // rows
        out_specs = [pl.BlockSpec((rows, D_MODEL), lambda i: (jnp.minimum(i, prompt_tiles - 1), 0)),
                     pl.BlockSpec((rows, D_MODEL), lambda i: (jnp.maximum(i - prompt_tiles, 0), 0))]
        out_shape = [jax.ShapeDtypeStruct((P_ROWS, D_MODEL), F32), jax.ShapeDtypeStruct((S_ROWS, D_MODEL), F32)]
    if emit_h:
        nl, nn = next_mod
        in_specs += [_mod_spec(nl, nn, 0, rows), _mod_spec(nl, nn, 1, rows)]
        args += [mods, mods]
        if h_token_major:
            out_specs.append(pl.BlockSpec((rows * LANE_TILES, 128), lambda i: (i, 0)))
            out_shape.append(jax.ShapeDtypeStruct((M_ROWS * LANE_TILES, 128), F32))
        else:
            out_specs.append(row)
            out_shape.append(jax.ShapeDtypeStruct((M_ROWS, D_MODEL), BF))
    if router:
        in_specs.append(pl.BlockSpec((None, D_MODEL, ROUTER_LANES), lambda i: (wr_layer, 0, 0)))
        args.append(wr)
        lane_out = pl.BlockSpec((rows, ROUTER_LANES), lambda i: (i, 0))
        out_specs += [lane_out, lane_out]
        out_shape += [jax.ShapeDtypeStruct((M_ROWS, ROUTER_LANES), jnp.int32),
                      jax.ShapeDtypeStruct((M_ROWS, ROUTER_LANES), F32)]
    return pl.pallas_call(
        functools.partial(_post_kernel, rows=rows, nblk=nblk, gathered=gathered, projected=projected,
                          emit_h=emit_h, router=router, h_token_major=h_token_major),
        grid=(nblk,),
        in_specs=in_specs,
        out_specs=out_specs,
        out_shape=out_shape,
        scratch_shapes=scratch,
        compiler_params=_params(("arbitrary",), 56),
        name=name,
    )(*args)


def _gather_kernel(idx_cur, idx_nxt, live_ref, src_hbm, out_ref, buf, sem, *, rows, nblk):
    i = pl.program_id(0)
    slot = i % 2
    live = live_ref[i] == 1

    def token_copy(tok, s, r):
        dst = pl.multiple_of((s * rows + r) * LANE_TILES, LANE_TILES)
        return pltpu.make_async_copy(src_hbm.at[tok], buf.at[pl.ds(dst, LANE_TILES)], sem.at[s])

    def issue(idx_ref, s):
        def body(r, carry):
            token_copy(idx_ref[0, r], s, r).start()
            return carry

        lax.fori_loop(0, rows, body, 0)

    @pl.when(jnp.logical_and(i == 0, live))
    def _():
        issue(idx_cur, 0)

    @pl.when(jnp.logical_and(i + 1 < nblk, live_ref[jnp.minimum(i + 1, nblk - 1)] == 1))
    def _():
        issue(idx_nxt, 1 - slot)

    @pl.when(live)
    def _():
        def wait(r, carry):
            token_copy(0, slot, r).wait()
            return carry

        lax.fori_loop(0, rows, wait, 0)

    @pl.when(jnp.logical_not(live))
    def _():
        out_ref[...] = jnp.zeros_like(out_ref)

    for static_slot in range(2):
        @pl.when(jnp.logical_and(live, slot == static_slot))
        def _(static_slot=static_slot):
            base = static_slot * rows * LANE_TILES
            for s in range(LANE_TILES):
                out_ref[:, s * 128:(s + 1) * 128] = buf[pl.ds(base + s, rows, stride=LANE_TILES), :].astype(BF)


def _gather_call(src, idx, live):
    rows = GATHER_ROWS
    nblk = idx.shape[0] // rows
    idx3 = idx.reshape(nblk, 1, rows)
    return pl.pallas_call(
        functools.partial(_gather_kernel, rows=rows, nblk=nblk),
        grid=(nblk,),
        in_specs=[
            pl.BlockSpec((None, 1, rows), lambda i: (i, 0, 0), memory_space=pltpu.SMEM),
            pl.BlockSpec((None, 1, rows), lambda i: (jnp.minimum(i + 1, nblk - 1), 0, 0),
                         memory_space=pltpu.SMEM),
            pl.BlockSpec(memory_space=pltpu.SMEM),
            pl.BlockSpec(memory_space=pl.ANY),
        ],
        out_specs=pl.BlockSpec((rows, D_MODEL), lambda i: (i, 0)),
        out_shape=jax.ShapeDtypeStruct((idx.shape[0], D_MODEL), BF),
        scratch_shapes=[pltpu.VMEM((2 * rows * LANE_TILES, 128), F32), pltpu.SemaphoreType.DMA((2,))],
        compiler_params=_params(("arbitrary",), 32),
        name="moe_gather",
    )(idx3, idx3, live, src)


def _dispatch_plan(te):
    flat_e = te[:, :TOP_K].reshape(-1)
    experts = jnp.arange(N_EXPERTS, dtype=jnp.int32)
    onehot = (flat_e[:, None] == experts[None, :]).astype(jnp.int32)
    csum = jnp.cumsum(onehot, axis=0)
    rank = jnp.sum(csum * onehot, axis=1) - 1
    counts = csum[-1]
    padded = (counts + MOE_BLK - 1) // MOE_BLK * MOE_BLK
    pad_end = jnp.cumsum(padded)
    pad_start = pad_end - padded
    dest = (jnp.sum(pad_start[None, :] * onehot, axis=1) + rank).astype(jnp.int32)
    tok_of_row = jnp.zeros((MOE_ROWS,), jnp.int32).at[dest].set(
        jnp.arange(N_ASSIGN, dtype=jnp.int32) // TOP_K)
    dest2 = dest.reshape(M_ROWS, TOP_K)

    valid_end = pad_start + counts

    later_used = jnp.logical_and(experts[None, :] > experts[:, None], (padded > 0)[None, :])
    nxt_of_e = jnp.min(jnp.where(later_used, experts[None, :], N_EXPERTS), axis=1)
    nxt_of_e = jnp.where(nxt_of_e == N_EXPERTS, -1, nxt_of_e)

    def plan(blk):
        ids = jnp.arange(MOE_ROWS // blk, dtype=jnp.int32)
        starts = ids * blk
        in_range = jnp.minimum(starts, pad_end[-1] - blk)
        block_e = jnp.sum((pad_end[None, :] <= in_range[:, None]).astype(jnp.int32), axis=1)
        onehot_e = (block_e[:, None] == experts[None, :]).astype(jnp.int32)
        valid = jnp.clip(jnp.sum(valid_end[None, :] * onehot_e, axis=1) - starts, 0, blk)
        live = (valid > 0).astype(jnp.int32)
        src = lax.cummax(jnp.where(live == 1, ids, 0), axis=0)
        nxt = jnp.sum(nxt_of_e[None, :] * onehot_e, axis=1)
        return block_e.astype(jnp.int32), src.astype(jnp.int32), live, nxt.astype(jnp.int32)

    return tok_of_row, dest2[:, 0], dest2[:, 1], plan


def kernel(x_prompt, x_sample, state_conv, state_pool, c_prompt, c_sample, ln_in_g, ln_in_b, w_ada, b_ada,
           w_mix_in, sgu_g, sgu_b, w_spatial, b_spatial, conv_w, conv_b, w_pool, pool_scale, w_branch, w_o,
           ln_g, ln_b, w_ffn_gate, w_ffn_up, w_ffn_down, w_router, w_exp_gate, w_exp_up, w_exp_down):
    c_all = jnp.concatenate(
        [c_sample, c_prompt, jnp.zeros((C_ROWS - BATCH - DEC_BATCH, D_MODEL), F32)], axis=0)
    mods = _ada_call(c_all, w_ada, b_ada)

    wsp_bf = w_spatial.astype(BF)
    bsp_full = jnp.repeat(b_spatial.transpose(0, 2, 1), SGU_HD, axis=-1)
    wv = jnp.repeat(w_spatial[:, :, :DEC_SEQ, :DEC_SEQ].transpose(0, 2, 3, 1), SGU_HD, axis=-1)
    bv = bsp_full[:, :DEC_SEQ]
    wpool_bf = w_pool.astype(BF)
    state_conv_t = state_conv.transpose(0, 2, 1, 3)
    state_pool_t = state_pool.transpose(0, 2, 1, 3)
    wr_pad = jnp.pad(w_router, ((0, 0), (0, 0), (0, ROUTER_LANES - N_EXPERTS)))

    x, h = _ln_in_call(x_prompt.reshape(P_ROWS, D_MODEL), x_sample.transpose(1, 0, 2).reshape(S_ROWS, D_MODEL),
                       ln_in_g, ln_in_b, mods)

    conv_p, conv_s, pool_p, pool_s, v_s = [], [], [], [], []
    for l in range(DEPTH):
        i = l // 2
        proj = _dense_mm(h, [w_mix_in], l, bm=1536, tn=1024, vmem_mib=48, name="mix_in")
        br, cp, pp = _mid_prompt_call(proj, l, sgu_g, sgu_b, wsp_bf, bsp_full, conv_w, conv_b, wpool_bf,
                                      pool_scale)
        br3, vs, cs, ps = _mid_sample_call(
            proj.reshape(M_ROWS // SLAB, SLAB, MIX_IN_COLS), br.reshape(M_ROWS // SLAB, SLAB, N_BRANCH * MIX_W),
            l, state_conv_t, state_pool_t, sgu_g, sgu_b, wv, bv, conv_w, conv_b, wpool_bf, pool_scale)
        conv_p.append(cp[:, 8 - (CONV_W - 1):, :])
        pool_p.append(pp[:, 16 - POOL_BUF:, :])
        conv_s.append(cs.transpose(1, 0, 2))
        pool_s.append(ps.transpose(1, 0, 2))
        v_s.append(vs.transpose(1, 0, 2))
        merged = _merge_call(br3.reshape(M_ROWS, N_BRANCH * MIX_W), proj, w_branch, l)
        next_mod = (l + 1, 0) if l + 1 < DEPTH else None
        if l % 2 == 0:
            x, h2 = _post_call(merged, x, mods, l, 0, ln_g, ln_b, (l, 1), proj=(w_o, l), name="out_proj_post")
            hid = _dense_mm(h2, [w_ffn_gate, w_ffn_up], i, bm=1536, tn=512, swiglu=True, out_dtype=BF,
                            vmem_mib=48, name="ffn_up")
            f = _dense_mm(hid, [w_ffn_down], i, bm=1024, tn=512, vmem_mib=56, name="ffn_down")
            outs = _post_call(f, x, mods, l, 1, ln_g, ln_b, next_mod, name="post_ffn")
        else:
            mix = _dense_mm(merged, [w_o], l, bm=1536, tn=1024, vmem_mib=48, name="out_proj")
            x, h2, te, tw = _post_call(mix, x, mods, l, 0, ln_g, ln_b, (l, 1), h_token_major=True, wr=wr_pad,
                                       wr_layer=i, name="post_mix_router")
            tok_of_row, dest0, dest1, plan = _dispatch_plan(te)
            rows = _gather_call(h2.reshape(M_ROWS, LANE_TILES, 128), tok_of_row, plan(GATHER_ROWS)[2])
            moe_plan = plan(MOE_BLK)
            hid = _gmm_call(rows, [w_exp_gate, w_exp_up], i, moe_plan, bm=MOE_BLK, tn=1024, swiglu=True,
                            out_dtype=BF, vmem_mib=48, name="moe_up")
            y = _gmm_call(hid, [w_exp_down], i, moe_plan, bm=MOE_BLK, tn=512, swiglu=False,
                          out_dtype=F32, vmem_mib=56, name="moe_down")
            outs = _post_call(y, x, mods, l, 1, ln_g, ln_b, next_mod, gather=(dest0, dest1, tw),
                              name="moe_combine")
        if next_mod is None:
            x_prompt_out, x_sample_out = outs
        else:
            x, h = outs

    y_prompt = x_prompt_out.reshape(BATCH, SEQ, D_MODEL)
    y_sample = x_sample_out.reshape(DEC_SEQ, DEC_BATCH, D_MODEL).transpose(1, 0, 2)
    return (y_prompt, y_sample, jnp.stack(conv_p, 0), jnp.stack(conv_s, 0), jnp.stack(pool_p, 0),
            jnp.stack(pool_s, 0), jnp.stack(v_s, 0))
```

```python
import functools

import jax
import jax.numpy as jnp
from jax import lax
from jax.experimental import pallas as pl
from jax.experimental.pallas import tpu as pltpu

D_MODEL = 2048
BATCH = 4
SEQ = 2048
DEPTH = 4
DEC_BATCH = 128
DEC_SEQ = 8
PAST_LEN = 16384
MIX_W = 1024
CHUNK = 128
SGU_HEADS = 8
SGU_HD = MIX_W // SGU_HEADS
CONV_W = 3
POOL_WINDOWS = (2, 4, 8, 16)
POOL_GROUPS = 4
POOL_GD = MIX_W // POOL_GROUPS
POOL_BUF = 15
N_BRANCH = 3
MIX_IN_COLS = 6 * MIX_W + N_BRANCH * D_MODEL
D_FF = 5632
N_EXPERTS = 8
TOP_K = 2
E_FF = 7168
DEEPNORM_ALPHA = (2 * DEPTH) ** 0.25
LN_EPS = 1e-5

P_ROWS = BATCH * SEQ
S_ROWS = DEC_SEQ * DEC_BATCH
M_ROWS = P_ROWS + S_ROWS
SLAB = DEC_BATCH
C_ROWS = 136
ROUTER_LANES = 128
MOE_BLK = 512
N_ASSIGN = M_ROWS * TOP_K
MOE_ROWS = N_ASSIGN + N_EXPERTS * MOE_BLK
GATHER_ROWS = 256
POST_ROWS = 256
LANE_TILES = D_MODEL // 128

BF = jnp.bfloat16
F32 = jnp.float32

_MIB = 1 << 20


def _params(semantics, vmem_mib):
    return pltpu.CompilerParams(dimension_semantics=semantics, vmem_limit_bytes=vmem_mib * _MIB)


def _dot(a, b):
    return jnp.dot(a, b, preferred_element_type=F32)


def _ln(x, g, b):
    mu = jnp.mean(x, axis=-1, keepdims=True)
    xc = x - mu
    var = jnp.mean(xc * xc, axis=-1, keepdims=True)
    return xc * lax.rsqrt(var + LN_EPS) * g + b


def _slab_bcast(fn, x, *slabs):
    rows, d = x.shape
    x3 = x.reshape(rows // SLAB, SLAB, d)
    return fn(x3, *[s[None] for s in slabs]).reshape(rows, d)


def _modulate(x, shift, scale):
    return _slab_bcast(lambda a, sh, sc: a * (1.0 + sc) + sh, x, shift, scale)


def _mod_spec(l, n, k, tm):
    return pl.BlockSpec((None, None, None, None, SLAB, D_MODEL),
                        lambda i: (l, n, k, jnp.minimum(i * tm // SEQ, BATCH), 0, 0))


def _ada_kernel(c_ref, w_ref, b_ref, o_ref):
    c = c_ref[...]
    s = (c * jax.nn.sigmoid(c)).astype(BF)
    ada = _dot(s, w_ref[...].astype(BF)) + b_ref[...]
    for g in range(BATCH):
        o_ref[g] = jnp.broadcast_to(ada[DEC_BATCH + g:DEC_BATCH + g + 1, :], o_ref.shape[1:])
    o_ref[BATCH] = ada[0:DEC_BATCH, :]


def _ada_call(c_all, w_ada, b_ada):
    tn = 1024
    per_part = D_MODEL // tn
    return pl.pallas_call(
        _ada_kernel,
        grid=(DEPTH, 2, 3 * per_part),
        in_specs=[
            pl.BlockSpec((C_ROWS, D_MODEL), lambda l, n, j: (0, 0)),
            pl.BlockSpec((None, None, D_MODEL, tn), lambda l, n, j: (l, n, 0, j)),
            pl.BlockSpec((None, None, 1, tn), lambda l, n, j: (l, n, 0, j)),
        ],
        out_specs=pl.BlockSpec((None, None, None, BATCH + 1, SLAB, tn),
                               lambda l, n, j: (l, n, j // per_part, 0, 0, j % per_part)),
        out_shape=jax.ShapeDtypeStruct((DEPTH, 2, 3, BATCH + 1, SLAB, D_MODEL), F32),
        compiler_params=_params(("arbitrary",) * 3, 40),
        name="ada",
    )(c_all, w_ada, b_ada.reshape(DEPTH, 2, 1, 3 * D_MODEL))


def _ln_in_kernel(xp_ref, xs_ref, g_ref, b_ref, sh_ref, sc_ref, xo_ref, ho_ref, *, prompt_tiles):
    def emit(x_ref):
        xn = _ln(x_ref[...], g_ref[...], b_ref[...])
        xo_ref[...] = xn
        ho_ref[...] = _modulate(xn, sh_ref[...], sc_ref[...]).astype(BF)

    i = pl.program_id(0)
    pl.when(i < prompt_tiles)(lambda: emit(xp_ref))
    pl.when(i >= prompt_tiles)(lambda: emit(xs_ref))


def _ln_in_call(x_prompt_rows, x_sample_rows, g, b, mods):
    tm = POST_ROWS
    prompt_tiles = P_ROWS // tm
    row = pl.BlockSpec((tm, D_MODEL), lambda i: (i, 0))
    vec = pl.BlockSpec((1, D_MODEL), lambda i: (0, 0))
    return pl.pallas_call(
        functools.partial(_ln_in_kernel, prompt_tiles=prompt_tiles),
        grid=(M_ROWS // tm,),
        in_specs=[
            pl.BlockSpec((tm, D_MODEL), lambda i: (jnp.minimum(i, prompt_tiles - 1), 0)),
            pl.BlockSpec((tm, D_MODEL), lambda i: (jnp.maximum(i - prompt_tiles, 0), 0)),
            vec, vec, _mod_spec(0, 0, 0, tm), _mod_spec(0, 0, 1, tm)],
        out_specs=[row, row],
        out_shape=[jax.ShapeDtypeStruct((M_ROWS, D_MODEL), F32), jax.ShapeDtypeStruct((M_ROWS, D_MODEL), BF)],
        compiler_params=_params(("arbitrary",), 40),
        name="ln_in",
    )(x_prompt_rows, x_sample_rows, g.reshape(1, D_MODEL), b.reshape(1, D_MODEL), mods, mods)


def _gmm_kernel(be_ref, src_ref, live_ref, seg_ref, nxe_ref, lhs_ref, *refs, n_w, swiglu, w_layer, tn, nj):
    del src_ref
    w_hbm = refs[:n_w]
    o_ref = refs[n_w]
    stage = refs[n_w + 1:2 * n_w + 1]
    wb_refs = refs[2 * n_w + 1:3 * n_w + 1]
    sem = refs[3 * n_w + 1]
    j = pl.program_id(0)
    b = pl.program_id(1)
    nb = pl.num_programs(1)

    def tile_copy(k, e, jj):
        col = pl.multiple_of(jj * tn, tn)
        return pltpu.make_async_copy(w_hbm[k].at[w_layer, e, :, pl.ds(col, tn)], stage[k], sem.at[k])

    def start_tiles(e, jj):
        for k in range(n_w):
            tile_copy(k, e, jj).start()

    def land_tiles(e, jj, slot):
        for k in range(n_w):
            tile_copy(k, e, jj).wait()
            wb_refs[k][slot] = stage[k][...].astype(BF)

    def advance(e, jj):
        n = nxe_ref[e]
        return jnp.where(n >= 0, n, be_ref[0]), jnp.where(n >= 0, jj, jj + 1)

    e_here = be_ref[b]
    slot = (j * (seg_ref[nb - 1] + 1) + seg_ref[b]) % 2
    e1, j1 = advance(e_here, j)
    e2, j2 = advance(e1, j1)
    last_of_segment = jnp.logical_or(b == nb - 1, be_ref[jnp.minimum(b + 1, nb - 1)] != e_here)
    stage_next = jnp.logical_and(last_of_segment, j1 < nj)

    @pl.when(jnp.logical_and(j == 0, b == 0))
    def _():
        start_tiles(e_here, 0)
        land_tiles(e_here, 0, 0)

        @pl.when(j1 < nj)
        def _():
            start_tiles(e1, j1)

    def compute():
        lhs = lhs_ref[...]
        if swiglu:
            a = _dot(lhs, wb_refs[0][slot])
            u = _dot(lhs, wb_refs[1][slot])
            o_ref[...] = (a * jax.nn.sigmoid(a) * u).astype(o_ref.dtype)
        else:
            o_ref[...] = _dot(lhs, wb_refs[0][slot]).astype(o_ref.dtype)

    def land_and_restart():
        land_tiles(e1, j1, 1 - slot)

        @pl.when(j2 < nj)
        def _():
            start_tiles(e2, j2)

    live = live_ref[b] == 1

    @pl.when(jnp.logical_and(live, stage_next))
    def _():
        compute()
        land_and_restart()

    @pl.when(jnp.logical_and(live, jnp.logical_not(stage_next)))
    def _():
        compute()

    @pl.when(jnp.logical_not(live))
    def _():
        o_ref[...] = jnp.zeros_like(o_ref)

    @pl.when(jnp.logical_and(jnp.logical_not(live), stage_next))
    def _():
        land_and_restart()


def _gmm_call(lhs, weights, w_layer, plan, *, bm, tn, swiglu, out_dtype, vmem_mib, name):
    rows, k = lhs.shape
    n = weights[0].shape[-1]
    n_w = len(weights)
    nj = n // tn
    grid_spec = pltpu.PrefetchScalarGridSpec(
        num_scalar_prefetch=5,
        grid=(nj, rows // bm),
        in_specs=[pl.BlockSpec((bm, k), lambda j, b, be, sb, lv, sg, nx: (sb[b], 0))]
        + [pl.BlockSpec(memory_space=pl.ANY)] * n_w,
        out_specs=pl.BlockSpec((bm, tn), lambda j, b, be, sb, lv, sg, nx: (b, j)),
        scratch_shapes=[pltpu.VMEM((k, tn), F32)] * n_w + [pltpu.VMEM((2, k, tn), BF)] * n_w
        + [pltpu.SemaphoreType.DMA((n_w,))],
    )
    return pl.pallas_call(
        functools.partial(_gmm_kernel, n_w=n_w, swiglu=swiglu, w_layer=w_layer, tn=tn, nj=nj),
        grid_spec=grid_spec,
        out_shape=jax.ShapeDtypeStruct((rows, n), out_dtype),
        compiler_params=_params(("arbitrary", "arbitrary"), vmem_mib),
        name=name,
    )(*plan, lhs, *weights)


def _dense_mm(lhs, w, w_layer, *, bm, tn, swiglu=False, out_dtype=F32, vmem_mib, name):
    nb = lhs.shape[0] // bm
    zeros = jnp.zeros((nb,), jnp.int32)
    plan = (zeros, jnp.arange(nb, dtype=jnp.int32), jnp.ones((nb,), jnp.int32), zeros,
            jnp.full((1,), -1, jnp.int32))
    weights = [wi.reshape(wi.shape[0], 1, *wi.shape[1:]) for wi in w]
    return _gmm_call(lhs, weights, w_layer, plan, bm=bm, tn=tn, swiglu=swiglu, out_dtype=out_dtype,
                     vmem_mib=vmem_mib, name=name)


def _mid_prompt_kernel(*refs, rows, tiles_per_seq):
    br_ref = refs[14]
    i = pl.program_id(0)

    @pl.when(i < P_ROWS // rows)
    def _():
        _mid_prompt_tile(*refs, rows=rows, tile_in_seq=i % tiles_per_seq)

    @pl.when(i >= P_ROWS // rows)
    def _():
        br_ref[...] = jnp.zeros_like(br_ref)


def _mid_prompt_tile(u_ref, v_ref, gb_ref, gc_ref, hb_ref, p_ref, sg_ref, sb_ref, wsp_ref, bsp_ref,
                     cw_ref, cb_ref, wpool_ref, pscale_ref,
                     br_ref, convo_ref, poolo_ref, zh_ref, ph_ref, *, rows, tile_in_seq):

    @pl.when(tile_in_seq == 0)
    def _():
        zh_ref[...] = jnp.zeros_like(zh_ref)
        ph_ref[...] = jnp.zeros_like(ph_ref)

    v = _ln(v_ref[...], sg_ref[...], sb_ref[...])
    vb = v.astype(BF)
    r_id = lax.broadcasted_iota(jnp.int32, (CHUNK, CHUNK), 0)
    c_id = lax.broadcasted_iota(jnp.int32, (CHUNK, CHUNK), 1)
    causal = c_id <= r_id
    w_heads = [jnp.where(causal, wsp_ref[h], jnp.zeros((CHUNK, CHUNK), BF)) for h in range(SGU_HEADS)]
    chunks = []
    for c in range(rows // CHUNK):
        heads = [
            _dot(w_heads[h], vb[c * CHUNK:(c + 1) * CHUNK, h * SGU_HD:(h + 1) * SGU_HD])
            for h in range(SGU_HEADS)
        ]
        chunks.append(jnp.concatenate(heads, axis=1) + bsp_ref[...])
    s = jnp.concatenate(chunks, axis=0)
    br_ref[:, 0:MIX_W] = (u_ref[...] * s).astype(BF)

    z = gc_ref[...] * hb_ref[...]
    ze = jnp.concatenate([zh_ref[...], z], axis=0)
    z1 = pltpu.roll(ze, 1, 0)[8:]
    z2 = pltpu.roll(ze, 2, 0)[8:]
    conv = cb_ref[...] + z2 * cw_ref[0:1, :]
    conv = conv + z1 * cw_ref[1:2, :]
    conv = conv + z * cw_ref[2:3, :]
    br_ref[:, MIX_W:2 * MIX_W] = (gb_ref[...] * conv).astype(BF)

    p = p_ref[...]
    pe = jnp.concatenate([ph_ref[...], p], axis=0)
    win = pe + pltpu.roll(pe, 1, 0)
    sums = [win[16:, 0:POOL_GD]]
    shift = 2
    for j in range(1, POOL_GROUPS):
        win = win[:, POOL_GD:]
        win = win + pltpu.roll(win, shift, 0)
        sums.append(win[16:, 0:POOL_GD])
        shift *= 2
    pos1 = (tile_in_seq * rows + 1 + lax.broadcasted_iota(jnp.int32, (rows, 1), 0)).astype(F32)
    mixed = []
    for j, w in enumerate(POOL_WINDOWS):
        mean = sums[j] / jnp.minimum(float(w), pos1)
        pooled = mean - p[:, j * POOL_GD:(j + 1) * POOL_GD]
        mixed.append(_dot(pooled.astype(BF), wpool_ref[j]))
    y_c = jnp.concatenate(mixed, axis=1) * pscale_ref[...]
    br_ref[:, 2 * MIX_W:3 * MIX_W] = y_c.astype(BF)

    z_tail = z[rows - 8:, :]
    p_tail = p[rows - 16:, :]
    zh_ref[...] = z_tail
    ph_ref[...] = p_tail
    convo_ref[...] = z_tail
    poolo_ref[...] = p_tail


def _mid_prompt_call(proj, l, sgu_g, sgu_b, wsp_bf, bsp_full, conv_w, conv_b, wpool_bf, pool_scale):
    rows = 512
    tiles_per_seq = SEQ // rows

    def col(c):
        return pl.BlockSpec((rows, MIX_W), lambda i, c=c: (i, c))

    vec = pl.BlockSpec((None, 1, MIX_W), lambda i: (l, 0, 0))
    in_specs = [col(c) for c in range(6)] + [
        vec, vec,
        pl.BlockSpec((None, SGU_HEADS, CHUNK, CHUNK), lambda i: (l, 0, 0, 0)),
        pl.BlockSpec((None, CHUNK, MIX_W), lambda i: (l, 0, 0)),
        pl.BlockSpec((None, CONV_W, MIX_W), lambda i: (l, 0, 0)),
        vec,
        pl.BlockSpec((None, POOL_GROUPS, POOL_GD, POOL_GD), lambda i: (l, 0, 0, 0)),
        vec,
    ]
    def seq_of(i):
        return jnp.minimum(i // tiles_per_seq, BATCH - 1)

    out_specs = [
        pl.BlockSpec((rows, N_BRANCH * MIX_W), lambda i: (i, 0)),
        pl.BlockSpec((None, 8, MIX_W), lambda i: (seq_of(i), 0, 0)),
        pl.BlockSpec((None, 16, MIX_W), lambda i: (seq_of(i), 0, 0)),
    ]
    out_shape = [
        jax.ShapeDtypeStruct((M_ROWS, N_BRANCH * MIX_W), BF),
        jax.ShapeDtypeStruct((BATCH, 8, MIX_W), F32),
        jax.ShapeDtypeStruct((BATCH, 16, MIX_W), F32),
    ]
    return pl.pallas_call(
        functools.partial(_mid_prompt_kernel, rows=rows, tiles_per_seq=tiles_per_seq),
        grid=(M_ROWS // rows,),
        in_specs=in_specs,
        out_specs=out_specs,
        out_shape=out_shape,
        scratch_shapes=[pltpu.VMEM((8, MIX_W), F32), pltpu.VMEM((16, MIX_W), F32)],
        compiler_params=_params(("arbitrary",), 56),
        name="mid_prompt",
    )(proj, proj, proj, proj, proj, proj, sgu_g.reshape(DEPTH, 1, MIX_W), sgu_b.reshape(DEPTH, 1, MIX_W),
      wsp_bf, bsp_full, conv_w, conv_b.reshape(DEPTH, 1, MIX_W), wpool_bf, pool_scale.reshape(DEPTH, 1, MIX_W))


def _mid_sample_kernel(u_ref, v_ref, gb_ref, gc_ref, hb_ref, p_ref, cs_ref, ps_ref, sg_ref, sb_ref, wv_ref,
                       bv_ref, cw_ref, cb_ref, wpool_ref, pscale_ref, br_in_ref,
                       br_ref, vo_ref, convo_ref, poolo_ref, *, bs):
    del br_in_ref
    sg = sg_ref[...]
    sb = sb_ref[...]
    v = [_ln(v_ref[t], sg, sb) for t in range(DEC_SEQ)]
    for t in range(DEC_SEQ):
        vo_ref[t] = v[t]
        acc = v[0] * wv_ref[t, 0:1, :]
        for s in range(1, t + 1):
            acc = acc + v[s] * wv_ref[t, s:s + 1, :]
        br_ref[t, :, 0:MIX_W] = (u_ref[t] * (acc + bv_ref[t:t + 1, :])).astype(BF)

    zc = [cs_ref[0], cs_ref[1]] + [gc_ref[t] * hb_ref[t] for t in range(DEC_SEQ)]
    for t in range(DEC_SEQ):
        conv = cb_ref[...] + zc[t] * cw_ref[0:1, :]
        conv = conv + zc[t + 1] * cw_ref[1:2, :]
        conv = conv + zc[t + 2] * cw_ref[2:3, :]
        br_ref[t, :, MIX_W:2 * MIX_W] = (gb_ref[t] * conv).astype(BF)
    convo_ref[0] = zc[DEC_SEQ]
    convo_ref[1] = zc[DEC_SEQ + 1]

    pc = [ps_ref[i] for i in range(POOL_BUF)] + [p_ref[t] for t in range(DEC_SEQ)]
    for i in range(POOL_BUF):
        poolo_ref[i] = pc[DEC_SEQ + i]
    for j, w in enumerate(POOL_WINDOWS):
        lanes = slice(j * POOL_GD, (j + 1) * POOL_GD)
        pooled = []
        for t in range(DEC_SEQ):
            end = POOL_BUF + t
            tot = pc[end][:, lanes]
            for k in range(1, w):
                tot = tot + pc[end - k][:, lanes]
            cnt = float(min(w, PAST_LEN + t + 1))
            pooled.append(tot / cnt - pc[end][:, lanes])
        pooled = jnp.concatenate(pooled, axis=0).astype(BF)
        mixed = _dot(pooled, wpool_ref[j]) * pscale_ref[:, lanes]
        for t in range(DEC_SEQ):
            br_ref[t, :, 2 * MIX_W + j * POOL_GD:2 * MIX_W + (j + 1) * POOL_GD] = (
                mixed[t * bs:(t + 1) * bs].astype(BF))


def _mid_sample_call(proj3, br3, l, state_conv_t, state_pool_t, sgu_g, sgu_b, wv, bv, conv_w, conv_b, wpool_bf,
                     pool_scale):
    bs = 32
    t_block = P_ROWS // SLAB // DEC_SEQ
    n_slabs = M_ROWS // SLAB

    def col(c):
        return pl.BlockSpec((DEC_SEQ, bs, MIX_W), lambda i, c=c: (t_block, i, c))

    vec = pl.BlockSpec((None, 1, MIX_W), lambda i: (l, 0, 0))
    in_specs = [col(c) for c in range(6)] + [
        pl.BlockSpec((None, CONV_W - 1, bs, MIX_W), lambda i: (l, 0, i, 0)),
        pl.BlockSpec((None, POOL_BUF, bs, MIX_W), lambda i: (l, 0, i, 0)),
        vec, vec,
        pl.BlockSpec((None, DEC_SEQ, DEC_SEQ, MIX_W), lambda i: (l, 0, 0, 0)),
        pl.BlockSpec((None, DEC_SEQ, MIX_W), lambda i: (l, 0, 0)),
        pl.BlockSpec((None, CONV_W, MIX_W), lambda i: (l, 0, 0)),
        vec,
        pl.BlockSpec((None, POOL_GROUPS, POOL_GD, POOL_GD), lambda i: (l, 0, 0, 0)),
        vec,
        pl.BlockSpec(memory_space=pl.ANY),
    ]
    out_specs = [
        pl.BlockSpec((DEC_SEQ, bs, N_BRANCH * MIX_W), lambda i: (t_block, i, 0)),
        pl.BlockSpec((DEC_SEQ, bs, MIX_W), lambda i: (0, i, 0)),
        pl.BlockSpec((CONV_W - 1, bs, MIX_W), lambda i: (0, i, 0)),
        pl.BlockSpec((POOL_BUF, bs, MIX_W), lambda i: (0, i, 0)),
    ]
    out_shape = [
        jax.ShapeDtypeStruct((n_slabs, SLAB, N_BRANCH * MIX_W), BF),
        jax.ShapeDtypeStruct((DEC_SEQ, DEC_BATCH, MIX_W), F32),
        jax.ShapeDtypeStruct((CONV_W - 1, DEC_BATCH, MIX_W), F32),
        jax.ShapeDtypeStruct((POOL_BUF, DEC_BATCH, MIX_W), F32),
    ]
    return pl.pallas_call(
        functools.partial(_mid_sample_kernel, bs=bs),
        grid=(DEC_BATCH // bs,),
        in_specs=in_specs,
        out_specs=out_specs,
        out_shape=out_shape,
        input_output_aliases={len(in_specs) - 1: 0},
        compiler_params=_params(("arbitrary",), 48),
        name="mid_sample",
    )(proj3, proj3, proj3, proj3, proj3, proj3, state_conv_t, state_pool_t,
      sgu_g.reshape(DEPTH, 1, MIX_W), sgu_b.reshape(DEPTH, 1, MIX_W), wv, bv, conv_w,
      conv_b.reshape(DEPTH, 1, MIX_W), wpool_bf, pool_scale.reshape(DEPTH, 1, MIX_W), br3)


def _merge_kernel(br_ref, w_ref, g0_ref, g1_ref, g2_ref, o_ref, wb_ref):
    @pl.when(pl.program_id(1) == 0)
    def _():
        wb_ref[...] = w_ref[...].astype(BF)

    acc = None
    for r, g_ref in enumerate((g0_ref, g1_ref, g2_ref)):
        term = jax.nn.sigmoid(g_ref[...]) * _dot(br_ref[:, r * MIX_W:(r + 1) * MIX_W], wb_ref[r])
        acc = term if acc is None else acc + term
    o_ref[...] = acc.astype(BF)


def _merge_call(br, proj, w_branch, l):
    tm, tn = 1024, 512
    gate_col0 = 6 * MIX_W // tn
    per_gate = D_MODEL // tn

    def gate(r):
        return pl.BlockSpec((tm, tn), lambda j, i, r=r: (i, gate_col0 + r * per_gate + j))

    return pl.pallas_call(
        _merge_kernel,
        grid=(D_MODEL // tn, M_ROWS // tm),
        in_specs=[
            pl.BlockSpec((tm, N_BRANCH * MIX_W), lambda j, i: (i, 0)),
            pl.BlockSpec((None, N_BRANCH, MIX_W, tn), lambda j, i: (l, 0, 0, j)),
            gate(0), gate(1), gate(2),
        ],
        out_specs=pl.BlockSpec((tm, tn), lambda j, i: (i, j)),
        out_shape=jax.ShapeDtypeStruct((M_ROWS, D_MODEL), BF),
        scratch_shapes=[pltpu.VMEM((N_BRANCH, MIX_W, tn), BF)],
        compiler_params=_params(("arbitrary", "arbitrary"), 56),
        name="merge",
    )(br, w_branch, proj, proj, proj)


def _router_top2(h, wr_ref, te_ref, tw_ref):
    rows = h.shape[0]
    logits = jnp.dot(h, wr_ref[...], preferred_element_type=F32, precision=lax.Precision.HIGHEST)
    lane = lax.broadcasted_iota(jnp.int32, (rows, ROUTER_LANES), 1)
    lane_f = lane.astype(F32)
    valid = lane < N_EXPERTS
    lg = jnp.where(valid, logits, -jnp.inf)
    ex = jnp.exp(lg - jnp.max(lg, axis=-1, keepdims=True))
    probs = ex / jnp.sum(ex, axis=-1, keepdims=True)
    cand = jnp.where(valid, probs, -1.0)
    p1 = jnp.max(cand, axis=-1, keepdims=True)
    i1 = jnp.min(jnp.where(cand == p1, lane_f, float(ROUTER_LANES)), axis=-1, keepdims=True)
    cand = jnp.where(lane_f == i1, -1.0, cand)
    p2 = jnp.max(cand, axis=-1, keepdims=True)
    i2 = jnp.min(jnp.where(cand == p2, lane_f, float(ROUTER_LANES)), axis=-1, keepdims=True)
    tot = p1 + p2
    te_ref[...] = jnp.where(lane == 0, i1, jnp.where(lane == 1, i2, 0.0)).astype(jnp.int32)
    tw_ref[...] = jnp.where(lane == 0, p1 / tot, jnp.where(lane == 1, p2 / tot, 0.0))


def _post_kernel(*refs, rows, nblk, gathered, projected, emit_h, router, h_token_major):
    it = iter(refs)
    if gathered:
        d0c, d1c, d0n, d1n, y_hbm, tw_in_ref = (next(it) for _ in range(6))
    else:
        f_ref = next(it)
    if projected:
        w_ref = next(it)
    x_ref, gate_ref, lng_ref, lnb_ref = (next(it) for _ in range(4))
    sh_ref = sc_ref = wr_ref = ho_ref = te_ref = tw_ref = None
    if emit_h:
        sh_ref, sc_ref = next(it), next(it)
    if router:
        wr_ref = next(it)
    xo_ref = next(it)
    xo_sample_ref = None if emit_h else next(it)
    if emit_h:
        ho_ref = next(it)
    if router:
        te_ref, tw_ref = next(it), next(it)

    i = pl.program_id(0)
    if projected:
        wb_ref = next(it)

        @pl.when(i == 0)
        def _():
            wb_ref[...] = w_ref[...].astype(BF)

    if gathered:
        buf, sem = next(it), next(it)
        slot = i % 2

        def issue(d0_ref, d1_ref, s):
            def body(r, carry):
                pltpu.make_async_copy(y_hbm.at[pl.ds(d0_ref[0, r], 1)], buf.at[s, 0, pl.ds(r, 1)], sem.at[s]).start()
                pltpu.make_async_copy(y_hbm.at[pl.ds(d1_ref[0, r], 1)], buf.at[s, 1, pl.ds(r, 1)], sem.at[s]).start()
                return carry

            lax.fori_loop(0, rows, body, 0)

        @pl.when(i == 0)
        def _():
            issue(d0c, d1c, 0)

        @pl.when(i + 1 < nblk)
        def _():
            issue(d0n, d1n, 1 - slot)

        def wait(r, carry):
            pltpu.make_async_copy(y_hbm.at[pl.ds(0, 1)], buf.at[slot, 0, pl.ds(r, 1)], sem.at[slot]).wait()
            pltpu.make_async_copy(y_hbm.at[pl.ds(0, 1)], buf.at[slot, 1, pl.ds(r, 1)], sem.at[slot]).wait()
            return carry

        lax.fori_loop(0, rows, wait, 0)
        f = buf[slot, 0] * tw_in_ref[:, 0:1] + buf[slot, 1] * tw_in_ref[:, 1:2]
    elif projected:
        f = _dot(f_ref[...], wb_ref[...])
    else:
        f = f_ref[...]

    y = _slab_bcast(lambda a, g: a * g, f, gate_ref[...]) + DEEPNORM_ALPHA * x_ref[...]
    xn = _ln(y, lng_ref[...], lnb_ref[...])
    if emit_h:
        xo_ref[...] = xn
    else:
        @pl.when(i < P_ROWS // rows)
        def _():
            xo_ref[...] = xn

        @pl.when(i >= P_ROWS // rows)
        def _():
            xo_sample_ref[...] = xn

    if emit_h:
        h = _modulate(xn, sh_ref[...], sc_ref[...])
        if h_token_major:
            for s in range(LANE_TILES):
                ho_ref[pl.ds(s, rows, stride=LANE_TILES), :] = h[:, s * 128:(s + 1) * 128]
        else:
            ho_ref[...] = h.astype(ho_ref.dtype)
        if router:
            _router_top2(h, wr_ref, te_ref, tw_ref)


def _post_call(src, x, mods, l, n, ln_g, ln_b, next_mod, *, h_token_major=False, wr=None, wr_layer=0,
               gather=None, proj=None, name):
    rows = POST_ROWS
    nblk = M_ROWS // rows
    emit_h = next_mod is not None
    router = wr is not None
    gathered = gather is not None
    projected = proj is not None
    row = pl.BlockSpec((rows, D_MODEL), lambda i: (i, 0))
    vec = pl.BlockSpec((None, None, 1, D_MODEL), lambda i: (l, n, 0, 0))
    in_specs, args, scratch = [], [], []
    if projected:
        w, w_layer = proj
        in_specs += [row, pl.BlockSpec((None, D_MODEL, D_MODEL), lambda i: (w_layer, 0, 0),
                                       pipeline_mode=pl.Buffered(1))]
        args += [src, w]
        scratch = [pltpu.VMEM((D_MODEL, D_MODEL), BF)]
    if gathered:
        dest0, dest1, tw = gather
        cur = pl.BlockSpec((None, 1, rows), lambda i: (i, 0, 0), memory_space=pltpu.SMEM)
        nxt = pl.BlockSpec((None, 1, rows), lambda i: (jnp.minimum(i + 1, nblk - 1), 0, 0),
                           memory_space=pltpu.SMEM)
        d0, d1 = dest0.reshape(nblk, 1, rows), dest1.reshape(nblk, 1, rows)
        in_specs += [cur, cur, nxt, nxt, pl.BlockSpec(memory_space=pl.ANY),
                     pl.BlockSpec((rows, ROUTER_LANES), lambda i: (i, 0))]
        args += [d0, d1, d0, d1, src, tw]
        scratch = [pltpu.VMEM((2, TOP_K, rows, D_MODEL), F32), pltpu.SemaphoreType.DMA((2,))]
    elif not projected:
        in_specs.append(row)
        args.append(src)
    in_specs += [row, _mod_spec(l, n, 2, rows), vec, vec]
    args += [x, mods, ln_g.reshape(DEPTH, 2, 1, D_MODEL), ln_b.reshape(DEPTH, 2, 1, D_MODEL)]
    if emit_h:
        out_specs = [row]
        out_shape = [jax.ShapeDtypeStruct((M_ROWS, D_MODEL), F32)]
    else:
        prompt_tiles = P_ROWS // rows
        out_specs = [pl.BlockSpec((rows, D_MODEL), lambda i: (jnp.minimum(i, prompt_tiles - 1), 0)),
                     pl.BlockSpec((rows, D_MODEL), lambda i: (jnp.maximum(i - prompt_tiles, 0), 0))]
        out_shape = [jax.ShapeDtypeStruct((P_ROWS, D_MODEL), F32), jax.ShapeDtypeStruct((S_ROWS, D_MODEL), F32)]
    if emit_h:
        nl, nn = next_mod
        in_specs += [_mod_spec(nl, nn, 0, rows), _mod_spec(nl, nn, 1, rows)]
        args += [mods, mods]
        if h_token_major:
            out_specs.append(pl.BlockSpec((rows * LANE_TILES, 128), lambda i: (i, 0)))
            out_shape.append(jax.ShapeDtypeStruct((M_ROWS * LANE_TILES, 128), F32))
        else:
            out_specs.append(row)
            out_shape.append(jax.ShapeDtypeStruct((M_ROWS, D_MODEL), BF))
    if router:
        in_specs.append(pl.BlockSpec((None, D_MODEL, ROUTER_LANES), lambda i: (wr_layer, 0, 0)))
        args.append(wr)
        lane_out = pl.BlockSpec((rows, ROUTER_LANES), lambda i: (i, 0))
        out_specs += [lane_out, lane_out]
        out_shape += [jax.ShapeDtypeStruct((M_ROWS, ROUTER_LANES), jnp.int32),
                      jax.ShapeDtypeStruct((M_ROWS, ROUTER_LANES), F32)]
    return pl.pallas_call(
        functools.partial(_post_kernel, rows=rows, nblk=nblk, gathered=gathered, projected=projected,
                          emit_h=emit_h, router=router, h_token_major=h_token_major),
        grid=(nblk,),
        in_specs=in_specs,
        out_specs=out_specs,
        out_shape=out_shape,
        scratch_shapes=scratch,
        compiler_params=_params(("arbitrary",), 56),
        name=name,
    )(*args)


def _gather_kernel(idx_cur, idx_nxt, live_ref, src_hbm, out_ref, buf, sem, *, rows, nblk):
    i = pl.program_id(0)
    slot = i % 2
    live = live_ref[i] == 1

    def token_copy(tok, s, r):
        dst = pl.multiple_of((s * rows + r) * LANE_TILES, LANE_TILES)
        return pltpu.make_async_copy(src_hbm.at[tok], buf.at[pl.ds(dst, LANE_TILES)], sem.at[s])

    def issue(idx_ref, s):
        def body(r, carry):
            token_copy(idx_ref[0, r], s, r).start()
            return carry

        lax.fori_loop(0, rows, body, 0)

    @pl.when(jnp.logical_and(i == 0, live))
    def _():
        issue(idx_cur, 0)

    @pl.when(jnp.logical_and(i + 1 < nblk, live_ref[jnp.minimum(i + 1, nblk - 1)] == 1))
    def _():
        issue(idx_nxt, 1 - slot)

    @pl.when(live)
    def _():
        def wait(r, carry):
            token_copy(0, slot, r).wait()
            return carry

        lax.fori_loop(0, rows, wait, 0)

    @pl.when(jnp.logical_not(live))
    def _():
        out_ref[...] = jnp.zeros_like(out_ref)

    for static_slot in range(2):
        @pl.when(jnp.logical_and(live, slot == static_slot))
        def _(static_slot=static_slot):
            base = static_slot * rows * LANE_TILES
            for s in range(LANE_TILES):
                out_ref[:, s * 128:(s + 1) * 128] = buf[pl.ds(base + s, rows, stride=LANE_TILES), :].astype(BF)


def _gather_call(src, idx, live):
    rows = GATHER_ROWS
    nblk = idx.shape[0] // rows
    idx3 = idx.reshape(nblk, 1, rows)
    return pl.pallas_call(
        functools.partial(_gather_kernel, rows=rows, nblk=nblk),
        grid=(nblk,),
        in_specs=[
            pl.BlockSpec((None, 1, rows), lambda i: (i, 0, 0), memory_space=pltpu.SMEM),
            pl.BlockSpec((None, 1, rows), lambda i: (jnp.minimum(i + 1, nblk - 1), 0, 0),
                         memory_space=pltpu.SMEM),
            pl.BlockSpec(memory_space=pltpu.SMEM),
            pl.BlockSpec(memory_space=pl.ANY),
        ],
        out_specs=pl.BlockSpec((rows, D_MODEL), lambda i: (i, 0)),
        out_shape=jax.ShapeDtypeStruct((idx.shape[0], D_MODEL), BF),
        scratch_shapes=[pltpu.VMEM((2 * rows * LANE_TILES, 128), F32), pltpu.SemaphoreType.DMA((2,))],
        compiler_params=_params(("arbitrary",), 32),
        name="moe_gather",
    )(idx3, idx3, live, src)


def _dispatch_plan(te):
    flat_e = te[:, :TOP_K].reshape(-1)
    experts = jnp.arange(N_EXPERTS, dtype=jnp.int32)
    onehot = (flat_e[:, None] == experts[None, :]).astype(jnp.int32)
    csum = jnp.cumsum(onehot, axis=0)
    rank = jnp.sum(csum * onehot, axis=1) - 1
    counts = csum[-1]
    padded = (counts + MOE_BLK - 1) // MOE_BLK * MOE_BLK
    pad_end = jnp.cumsum(padded)
    pad_start = pad_end - padded
    dest = (jnp.sum(pad_start[None, :] * onehot, axis=1) + rank).astype(jnp.int32)
    tok_of_row = jnp.zeros((MOE_ROWS,), jnp.int32).at[dest].set(
        jnp.arange(N_ASSIGN, dtype=jnp.int32) // TOP_K)
    dest2 = dest.reshape(M_ROWS, TOP_K)

    valid_end = pad_start + counts

    later_used = jnp.logical_and(experts[None, :] > experts[:, None], (padded > 0)[None, :])
    nxt_of_e = jnp.min(jnp.where(later_used, experts[None, :], N_EXPERTS), axis=1)
    nxt_of_e = jnp.where(nxt_of_e == N_EXPERTS, -1, nxt_of_e)

    def plan(blk):
        ids = jnp.arange(MOE_ROWS // blk, dtype=jnp.int32)
        starts = ids * blk
        in_range = jnp.minimum(starts, pad_end[-1] - blk)
        block_e = jnp.sum((pad_end[None, :] <= in_range[:, None]).astype(jnp.int32), axis=1)
        onehot_e = (block_e[:, None] == experts[None, :]).astype(jnp.int32)
        valid = jnp.clip(jnp.sum(valid_end[None, :] * onehot_e, axis=1) - starts, 0, blk)
        live = (valid > 0).astype(jnp.int32)
        src = lax.cummax(jnp.where(live == 1, ids, 0), axis=0)
        new_run = jnp.concatenate([jnp.zeros((1,), jnp.int32), (block_e[1:] != block_e[:-1]).astype(jnp.int32)])
        seg = jnp.cumsum(new_run)
        return (block_e.astype(jnp.int32), src.astype(jnp.int32), live, seg.astype(jnp.int32),
                nxt_of_e.astype(jnp.int32))

    return tok_of_row, dest2[:, 0], dest2[:, 1], plan


def kernel(x_prompt, x_sample, state_conv, state_pool, c_prompt, c_sample, ln_in_g, ln_in_b, w_ada, b_ada,
           w_mix_in, sgu_g, sgu_b, w_spatial, b_spatial, conv_w, conv_b, w_pool, pool_scale, w_branch, w_o,
           ln_g, ln_b, w_ffn_gate, w_ffn_up, w_ffn_down, w_router, w_exp_gate, w_exp_up, w_exp_down):
    c_all = jnp.concatenate(
        [c_sample, c_prompt, jnp.zeros((C_ROWS - BATCH - DEC_BATCH, D_MODEL), F32)], axis=0)
    mods = _ada_call(c_all, w_ada, b_ada)

    wsp_bf = w_spatial.astype(BF)
    bsp_full = jnp.repeat(b_spatial.transpose(0, 2, 1), SGU_HD, axis=-1)
    wv = jnp.repeat(w_spatial[:, :, :DEC_SEQ, :DEC_SEQ].transpose(0, 2, 3, 1), SGU_HD, axis=-1)
    bv = bsp_full[:, :DEC_SEQ]
    wpool_bf = w_pool.astype(BF)
    state_conv_t = state_conv.transpose(0, 2, 1, 3)
    state_pool_t = state_pool.transpose(0, 2, 1, 3)
    wr_pad = jnp.pad(w_router, ((0, 0), (0, 0), (0, ROUTER_LANES - N_EXPERTS)))

    x, h = _ln_in_call(x_prompt.reshape(P_ROWS, D_MODEL), x_sample.transpose(1, 0, 2).reshape(S_ROWS, D_MODEL),
                       ln_in_g, ln_in_b, mods)

    conv_p, conv_s, pool_p, pool_s, v_s = [], [], [], [], []
    for l in range(DEPTH):
        i = l // 2
        proj = _dense_mm(h, [w_mix_in], l, bm=1536, tn=1024, vmem_mib=56, name="mix_in")
        br, cp, pp = _mid_prompt_call(proj, l, sgu_g, sgu_b, wsp_bf, bsp_full, conv_w, conv_b, wpool_bf,
                                      pool_scale)
        br3, vs, cs, ps = _mid_sample_call(
            proj.reshape(M_ROWS // SLAB, SLAB, MIX_IN_COLS), br.reshape(M_ROWS // SLAB, SLAB, N_BRANCH * MIX_W),
            l, state_conv_t, state_pool_t, sgu_g, sgu_b, wv, bv, conv_w, conv_b, wpool_bf, pool_scale)
        conv_p.append(cp[:, 8 - (CONV_W - 1):, :])
        pool_p.append(pp[:, 16 - POOL_BUF:, :])
        conv_s.append(cs.transpose(1, 0, 2))
        pool_s.append(ps.transpose(1, 0, 2))
        v_s.append(vs.transpose(1, 0, 2))
        merged = _merge_call(br3.reshape(M_ROWS, N_BRANCH * MIX_W), proj, w_branch, l)
        next_mod = (l + 1, 0) if l + 1 < DEPTH else None
        if l % 2 == 0:
            x, h2 = _post_call(merged, x, mods, l, 0, ln_g, ln_b, (l, 1), proj=(w_o, l), name="out_proj_post")
            hid = _dense_mm(h2, [w_ffn_gate, w_ffn_up], i, bm=1024, tn=512, swiglu=True, out_dtype=BF,
                            vmem_mib=48, name="ffn_up")
            f = _dense_mm(hid, [w_ffn_down], i, bm=512, tn=512, vmem_mib=52, name="ffn_down")
            outs = _post_call(f, x, mods, l, 1, ln_g, ln_b, next_mod, name="post_ffn")
        else:
            mix = _dense_mm(merged, [w_o], l, bm=1536, tn=1024, vmem_mib=56, name="out_proj")
            x, h2, te, tw = _post_call(mix, x, mods, l, 0, ln_g, ln_b, (l, 1), h_token_major=True, wr=wr_pad,
                                       wr_layer=i, name="post_mix_router")
            tok_of_row, dest0, dest1, plan = _dispatch_plan(te)
            rows = _gather_call(h2.reshape(M_ROWS, LANE_TILES, 128), tok_of_row, plan(GATHER_ROWS)[2])
            moe_plan = plan(MOE_BLK)
            hid = _gmm_call(rows, [w_exp_gate, w_exp_up], i, moe_plan, bm=MOE_BLK, tn=1024, swiglu=True,
                            out_dtype=BF, vmem_mib=56, name="moe_up")
            y = _gmm_call(hid, [w_exp_down], i, moe_plan, bm=MOE_BLK, tn=512, swiglu=False,
                          out_dtype=F32, vmem_mib=60, name="moe_down")
            outs = _post_call(y, x, mods, l, 1, ln_g, ln_b, next_mod, gather=(dest0, dest1, tw),
                              name="moe_combine")
        if next_mod is None:
            x_prompt_out, x_sample_out = outs
        else:
            x, h = outs

    y_prompt = x_prompt_out.reshape(BATCH, SEQ, D_MODEL)
    y_sample = x_sample_out.reshape(DEC_SEQ, DEC_BATCH, D_MODEL).transpose(1, 0, 2)
    return (y_prompt, y_sample, jnp.stack(conv_p, 0), jnp.stack(conv_s, 0), jnp.stack(pool_p, 0),
            jnp.stack(pool_s, 0), jnp.stack(v_s, 0))
```

```python
import functools

import jax
import jax.numpy as jnp
from jax import lax
from jax.experimental import pallas as pl
from jax.experimental.pallas import tpu as pltpu

D_MODEL = 2048
BATCH = 4
SEQ = 2048
DEPTH = 4
DEC_BATCH = 128
DEC_SEQ = 8
PAST_LEN = 16384
MIX_W = 1024
CHUNK = 128
SGU_HEADS = 8
SGU_HD = MIX_W // SGU_HEADS
CONV_W = 3
POOL_WINDOWS = (2, 4, 8, 16)
POOL_GROUPS = 4
POOL_GD = MIX_W // POOL_GROUPS
POOL_BUF = 15
N_BRANCH = 3
MIX_IN_COLS = 6 * MIX_W + N_BRANCH * D_MODEL
D_FF = 5632
N_EXPERTS = 8
TOP_K = 2
E_FF = 7168
DEEPNORM_ALPHA = (2 * DEPTH) ** 0.25
LN_EPS = 1e-5

P_ROWS = BATCH * SEQ
S_ROWS = DEC_SEQ * DEC_BATCH
M_ROWS = P_ROWS + S_ROWS
SLAB = DEC_BATCH
C_ROWS = 136
ROUTER_LANES = 128
MOE_BLK = 512
N_ASSIGN = M_ROWS * TOP_K
MOE_ROWS = N_ASSIGN + N_EXPERTS * MOE_BLK
GATHER_ROWS = 256
POST_ROWS = 256
LANE_TILES = D_MODEL // 128

BF = jnp.bfloat16
F32 = jnp.float32

_MIB = 1 << 20


def _params(semantics, vmem_mib):
    return pltpu.CompilerParams(dimension_semantics=semantics, vmem_limit_bytes=vmem_mib * _MIB)


def _dot(a, b):
    return jnp.dot(a, b, preferred_element_type=F32)


def _ln(x, g, b):
    mu = jnp.mean(x, axis=-1, keepdims=True)
    xc = x - mu
    var = jnp.mean(xc * xc, axis=-1, keepdims=True)
    return xc * lax.rsqrt(var + LN_EPS) * g + b


def _slab_bcast(fn, x, *slabs):
    rows, d = x.shape
    x3 = x.reshape(rows // SLAB, SLAB, d)
    return fn(x3, *[s[None] for s in slabs]).reshape(rows, d)


def _modulate(x, shift, scale):
    return _slab_bcast(lambda a, sh, sc: a * (1.0 + sc) + sh, x, shift, scale)


def _mod_spec(l, n, k, tm):
    return pl.BlockSpec((None, None, None, None, SLAB, D_MODEL),
                        lambda i: (l, n, k, jnp.minimum(i * tm // SEQ, BATCH), 0, 0))


def _ada_kernel(c_ref, w_ref, b_ref, o_ref):
    c = c_ref[...]
    s = (c * jax.nn.sigmoid(c)).astype(BF)
    ada = _dot(s, w_ref[...].astype(BF)) + b_ref[...]
    for g in range(BATCH):
        o_ref[g] = jnp.broadcast_to(ada[DEC_BATCH + g:DEC_BATCH + g + 1, :], o_ref.shape[1:])
    o_ref[BATCH] = ada[0:DEC_BATCH, :]


def _ada_call(c_all, w_ada, b_ada):
    tn = 1024
    per_part = D_MODEL // tn
    return pl.pallas_call(
        _ada_kernel,
        grid=(DEPTH, 2, 3 * per_part),
        in_specs=[
            pl.BlockSpec((C_ROWS, D_MODEL), lambda l, n, j: (0, 0)),
            pl.BlockSpec((None, None, D_MODEL, tn), lambda l, n, j: (l, n, 0, j)),
            pl.BlockSpec((None, None, 1, tn), lambda l, n, j: (l, n, 0, j)),
        ],
        out_specs=pl.BlockSpec((None, None, None, BATCH + 1, SLAB, tn),
                               lambda l, n, j: (l, n, j // per_part, 0, 0, j % per_part)),
        out_shape=jax.ShapeDtypeStruct((DEPTH, 2, 3, BATCH + 1, SLAB, D_MODEL), F32),
        compiler_params=_params(("arbitrary",) * 3, 40),
        name="ada",
    )(c_all, w_ada, b_ada.reshape(DEPTH, 2, 1, 3 * D_MODEL))


def _ln_in_kernel(xp_ref, xs_ref, g_ref, b_ref, sh_ref, sc_ref, xo_ref, ho_ref, *, prompt_tiles):
    def emit(x_ref):
        xn = _ln(x_ref[...], g_ref[...], b_ref[...])
        xo_ref[...] = xn
        ho_ref[...] = _modulate(xn, sh_ref[...], sc_ref[...]).astype(BF)

    i = pl.program_id(0)
    pl.when(i < prompt_tiles)(lambda: emit(xp_ref))
    pl.when(i >= prompt_tiles)(lambda: emit(xs_ref))


def _ln_in_call(x_prompt_rows, x_sample_rows, g, b, mods):
    tm = POST_ROWS
    prompt_tiles = P_ROWS // tm
    row = pl.BlockSpec((tm, D_MODEL), lambda i: (i, 0))
    vec = pl.BlockSpec((1, D_MODEL), lambda i: (0, 0))
    return pl.pallas_call(
        functools.partial(_ln_in_kernel, prompt_tiles=prompt_tiles),
        grid=(M_ROWS // tm,),
        in_specs=[
            pl.BlockSpec((tm, D_MODEL), lambda i: (jnp.minimum(i, prompt_tiles - 1), 0)),
            pl.BlockSpec((tm, D_MODEL), lambda i: (jnp.maximum(i - prompt_tiles, 0), 0)),
            vec, vec, _mod_spec(0, 0, 0, tm), _mod_spec(0, 0, 1, tm)],
        out_specs=[row, row],
        out_shape=[jax.ShapeDtypeStruct((M_ROWS, D_MODEL), F32), jax.ShapeDtypeStruct((M_ROWS, D_MODEL), BF)],
        compiler_params=_params(("arbitrary",), 40),
        name="ln_in",
    )(x_prompt_rows, x_sample_rows, g.reshape(1, D_MODEL), b.reshape(1, D_MODEL), mods, mods)


def _gmm_kernel(be_ref, src_ref, live_ref, nxt_ref, lhs_ref, *refs, n_w, swiglu, w_layer, tn, nj):
    del src_ref
    w_hbm = refs[:n_w]
    o_ref = refs[n_w]
    stage = refs[n_w + 1:2 * n_w + 1]
    wb_refs = refs[2 * n_w + 1:3 * n_w + 1]
    sem = refs[3 * n_w + 1]
    j = pl.program_id(0)
    b = pl.program_id(1)

    def tile_copy(k, e, jj):
        col = pl.multiple_of(jj * tn, tn)
        return pltpu.make_async_copy(w_hbm[k].at[w_layer, e, :, pl.ds(col, tn)], stage[k], sem.at[k])

    def start_tiles(e, jj):
        for k in range(n_w):
            tile_copy(k, e, jj).start()

    e_here = be_ref[b]
    changed = jnp.logical_or(b == 0, e_here != be_ref[jnp.maximum(b - 1, 0)])

    @pl.when(jnp.logical_and(j == 0, b == 0))
    def _():
        start_tiles(e_here, 0)

    @pl.when(changed)
    def _():
        for k in range(n_w):
            tile_copy(k, e_here, j).wait()
            wb_refs[k][...] = stage[k][...].astype(BF)
        nxt = nxt_ref[b]

        @pl.when(nxt >= 0)
        def _():
            start_tiles(nxt, j)

        @pl.when(jnp.logical_and(nxt < 0, j + 1 < nj))
        def _():
            start_tiles(be_ref[0], j + 1)

    live = live_ref[b] == 1

    @pl.when(live)
    def _():
        lhs = lhs_ref[...]
        if swiglu:
            a = _dot(lhs, wb_refs[0][...])
            u = _dot(lhs, wb_refs[1][...])
            o_ref[...] = (a * jax.nn.sigmoid(a) * u).astype(o_ref.dtype)
        else:
            o_ref[...] = _dot(lhs, wb_refs[0][...]).astype(o_ref.dtype)

    @pl.when(jnp.logical_not(live))
    def _():
        o_ref[...] = jnp.zeros_like(o_ref)


def _gmm_call(lhs, weights, w_layer, plan, *, bm, tn, swiglu, out_dtype, vmem_mib, name):
    rows, k = lhs.shape
    n = weights[0].shape[-1]
    n_w = len(weights)
    nj = n // tn
    grid_spec = pltpu.PrefetchScalarGridSpec(
        num_scalar_prefetch=4,
        grid=(nj, rows // bm),
        in_specs=[pl.BlockSpec((bm, k), lambda j, b, be, sb, lv, nx: (sb[b], 0))]
        + [pl.BlockSpec(memory_space=pl.ANY)] * n_w,
        out_specs=pl.BlockSpec((bm, tn), lambda j, b, be, sb, lv, nx: (b, j)),
        scratch_shapes=[pltpu.VMEM((k, tn), F32)] * n_w + [pltpu.VMEM((k, tn), BF)] * n_w
        + [pltpu.SemaphoreType.DMA((n_w,))],
    )
    return pl.pallas_call(
        functools.partial(_gmm_kernel, n_w=n_w, swiglu=swiglu, w_layer=w_layer, tn=tn, nj=nj),
        grid_spec=grid_spec,
        out_shape=jax.ShapeDtypeStruct((rows, n), out_dtype),
        compiler_params=_params(("arbitrary", "arbitrary"), vmem_mib),
        name=name,
    )(*plan, lhs, *weights)


def _dense_mm(lhs, w, w_layer, *, bm, tn, swiglu=False, out_dtype=F32, vmem_mib, name):
    nb = lhs.shape[0] // bm
    plan = (jnp.zeros((nb,), jnp.int32), jnp.arange(nb, dtype=jnp.int32), jnp.ones((nb,), jnp.int32),
            jnp.full((nb,), -1, jnp.int32))
    weights = [wi.reshape(wi.shape[0], 1, *wi.shape[1:]) for wi in w]
    return _gmm_call(lhs, weights, w_layer, plan, bm=bm, tn=tn, swiglu=swiglu, out_dtype=out_dtype,
                     vmem_mib=vmem_mib, name=name)


def _mid_prompt_kernel(*refs, rows, tiles_per_seq):
    br_ref = refs[14]
    i = pl.program_id(0)

    @pl.when(i < P_ROWS // rows)
    def _():
        _mid_prompt_tile(*refs, rows=rows, tile_in_seq=i % tiles_per_seq)

    @pl.when(i >= P_ROWS // rows)
    def _():
        br_ref[...] = jnp.zeros_like(br_ref)


def _mid_prompt_tile(u_ref, v_ref, gb_ref, gc_ref, hb_ref, p_ref, sg_ref, sb_ref, wsp_ref, bsp_ref,
                     cw_ref, cb_ref, wpool_ref, pscale_ref,
                     br_ref, convo_ref, poolo_ref, zh_ref, ph_ref, *, rows, tile_in_seq):

    @pl.when(tile_in_seq == 0)
    def _():
        zh_ref[...] = jnp.zeros_like(zh_ref)
        ph_ref[...] = jnp.zeros_like(ph_ref)

    v = _ln(v_ref[...], sg_ref[...], sb_ref[...])
    vb = v.astype(BF)
    r_id = lax.broadcasted_iota(jnp.int32, (CHUNK, CHUNK), 0)
    c_id = lax.broadcasted_iota(jnp.int32, (CHUNK, CHUNK), 1)
    causal = c_id <= r_id
    w_heads = [jnp.where(causal, wsp_ref[h], jnp.zeros((CHUNK, CHUNK), BF)) for h in range(SGU_HEADS)]
    chunks = []
    for c in range(rows // CHUNK):
        heads = [
            _dot(w_heads[h], vb[c * CHUNK:(c + 1) * CHUNK, h * SGU_HD:(h + 1) * SGU_HD])
            for h in range(SGU_HEADS)
        ]
        chunks.append(jnp.concatenate(heads, axis=1) + bsp_ref[...])
    s = jnp.concatenate(chunks, axis=0)
    br_ref[:, 0:MIX_W] = (u_ref[...] * s).astype(BF)

    z = gc_ref[...] * hb_ref[...]
    ze = jnp.concatenate([zh_ref[...], z], axis=0)
    z1 = pltpu.roll(ze, 1, 0)[8:]
    z2 = pltpu.roll(ze, 2, 0)[8:]
    conv = cb_ref[...] + z2 * cw_ref[0:1, :]
    conv = conv + z1 * cw_ref[1:2, :]
    conv = conv + z * cw_ref[2:3, :]
    br_ref[:, MIX_W:2 * MIX_W] = (gb_ref[...] * conv).astype(BF)

    p = p_ref[...]
    pe = jnp.concatenate([ph_ref[...], p], axis=0)
    win = pe + pltpu.roll(pe, 1, 0)
    sums = [win[16:, 0:POOL_GD]]
    shift = 2
    for j in range(1, POOL_GROUPS):
        win = win[:, POOL_GD:]
        win = win + pltpu.roll(win, shift, 0)
        sums.append(win[16:, 0:POOL_GD])
        shift *= 2
    pos1 = (tile_in_seq * rows + 1 + lax.broadcasted_iota(jnp.int32, (rows, 1), 0)).astype(F32)
    mixed = []
    for j, w in enumerate(POOL_WINDOWS):
        mean = sums[j] / jnp.minimum(float(w), pos1)
        pooled = mean - p[:, j * POOL_GD:(j + 1) * POOL_GD]
        mixed.append(_dot(pooled.astype(BF), wpool_ref[j]))
    y_c = jnp.concatenate(mixed, axis=1) * pscale_ref[...]
    br_ref[:, 2 * MIX_W:3 * MIX_W] = y_c.astype(BF)

    z_tail = z[rows - 8:, :]
    p_tail = p[rows - 16:, :]
    zh_ref[...] = z_tail
    ph_ref[...] = p_tail
    convo_ref[...] = z_tail
    poolo_ref[...] = p_tail


def _mid_prompt_call(proj, l, sgu_g, sgu_b, wsp_bf, bsp_full, conv_w, conv_b, wpool_bf, pool_scale):
    rows = 512
    tiles_per_seq = SEQ // rows

    def col(c):
        return pl.BlockSpec((rows, MIX_W), lambda i, c=c: (i, c))

    vec = pl.BlockSpec((None, 1, MIX_W), lambda i: (l, 0, 0))
    in_specs = [col(c) for c in range(6)] + [
        vec, vec,
        pl.BlockSpec((None, SGU_HEADS, CHUNK, CHUNK), lambda i: (l, 0, 0, 0)),
        pl.BlockSpec((None, CHUNK, MIX_W), lambda i: (l, 0, 0)),
        pl.BlockSpec((None, CONV_W, MIX_W), lambda i: (l, 0, 0)),
        vec,
        pl.BlockSpec((None, POOL_GROUPS, POOL_GD, POOL_GD), lambda i: (l, 0, 0, 0)),
        vec,
    ]
    def seq_of(i):
        return jnp.minimum(i // tiles_per_seq, BATCH - 1)

    out_specs = [
        pl.BlockSpec((rows, N_BRANCH * MIX_W), lambda i: (i, 0)),
        pl.BlockSpec((None, 8, MIX_W), lambda i: (seq_of(i), 0, 0)),
        pl.BlockSpec((None, 16, MIX_W), lambda i: (seq_of(i), 0, 0)),
    ]
    out_shape = [
        jax.ShapeDtypeStruct((M_ROWS, N_BRANCH * MIX_W), BF),
        jax.ShapeDtypeStruct((BATCH, 8, MIX_W), F32),
        jax.ShapeDtypeStruct((BATCH, 16, MIX_W), F32),
    ]
    return pl.pallas_call(
        functools.partial(_mid_prompt_kernel, rows=rows, tiles_per_seq=tiles_per_seq),
        grid=(M_ROWS // rows,),
        in_specs=in_specs,
        out_specs=out_specs,
        out_shape=out_shape,
        scratch_shapes=[pltpu.VMEM((8, MIX_W), F32), pltpu.VMEM((16, MIX_W), F32)],
        compiler_params=_params(("arbitrary",), 56),
        name="mid_prompt",
    )(proj, proj, proj, proj, proj, proj, sgu_g.reshape(DEPTH, 1, MIX_W), sgu_b.reshape(DEPTH, 1, MIX_W),
      wsp_bf, bsp_full, conv_w, conv_b.reshape(DEPTH, 1, MIX_W), wpool_bf, pool_scale.reshape(DEPTH, 1, MIX_W))


def _mid_sample_kernel(u_ref, v_ref, gb_ref, gc_ref, hb_ref, p_ref, cs_ref, ps_ref, sg_ref, sb_ref, wv_ref,
                       bv_ref, cw_ref, cb_ref, wpool_ref, pscale_ref, br_in_ref,
                       br_ref, vo_ref, convo_ref, poolo_ref, *, bs):
    del br_in_ref
    sg = sg_ref[...]
    sb = sb_ref[...]
    v = [_ln(v_ref[t], sg, sb) for t in range(DEC_SEQ)]
    for t in range(DEC_SEQ):
        vo_ref[t] = v[t]
        acc = v[0] * wv_ref[t, 0:1, :]
        for s in range(1, t + 1):
            acc = acc + v[s] * wv_ref[t, s:s + 1, :]
        br_ref[t, :, 0:MIX_W] = (u_ref[t] * (acc + bv_ref[t:t + 1, :])).astype(BF)

    zc = [cs_ref[0], cs_ref[1]] + [gc_ref[t] * hb_ref[t] for t in range(DEC_SEQ)]
    for t in range(DEC_SEQ):
        conv = cb_ref[...] + zc[t] * cw_ref[0:1, :]
        conv = conv + zc[t + 1] * cw_ref[1:2, :]
        conv = conv + zc[t + 2] * cw_ref[2:3, :]
        br_ref[t, :, MIX_W:2 * MIX_W] = (gb_ref[t] * conv).astype(BF)
    convo_ref[0] = zc[DEC_SEQ]
    convo_ref[1] = zc[DEC_SEQ + 1]

    pc = [ps_ref[i] for i in range(POOL_BUF)] + [p_ref[t] for t in range(DEC_SEQ)]
    for i in range(POOL_BUF):
        poolo_ref[i] = pc[DEC_SEQ + i]
    for j, w in enumerate(POOL_WINDOWS):
        lanes = slice(j * POOL_GD, (j + 1) * POOL_GD)
        pooled = []
        for t in range(DEC_SEQ):
            end = POOL_BUF + t
            tot = pc[end][:, lanes]
            for k in range(1, w):
                tot = tot + pc[end - k][:, lanes]
            cnt = float(min(w, PAST_LEN + t + 1))
            pooled.append(tot / cnt - pc[end][:, lanes])
        pooled = jnp.concatenate(pooled, axis=0).astype(BF)
        mixed = _dot(pooled, wpool_ref[j]) * pscale_ref[:, lanes]
        for t in range(DEC_SEQ):
            br_ref[t, :, 2 * MIX_W + j * POOL_GD:2 * MIX_W + (j + 1) * POOL_GD] = (
                mixed[t * bs:(t + 1) * bs].astype(BF))


def _mid_sample_call(proj3, br3, l, state_conv_t, state_pool_t, sgu_g, sgu_b, wv, bv, conv_w, conv_b, wpool_bf,
                     pool_scale):
    bs = 32
    t_block = P_ROWS // SLAB // DEC_SEQ
    n_slabs = M_ROWS // SLAB

    def col(c):
        return pl.BlockSpec((DEC_SEQ, bs, MIX_W), lambda i, c=c: (t_block, i, c))

    vec = pl.BlockSpec((None, 1, MIX_W), lambda i: (l, 0, 0))
    in_specs = [col(c) for c in range(6)] + [
        pl.BlockSpec((None, CONV_W - 1, bs, MIX_W), lambda i: (l, 0, i, 0)),
        pl.BlockSpec((None, POOL_BUF, bs, MIX_W), lambda i: (l, 0, i, 0)),
        vec, vec,
        pl.BlockSpec((None, DEC_SEQ, DEC_SEQ, MIX_W), lambda i: (l, 0, 0, 0)),
        pl.BlockSpec((None, DEC_SEQ, MIX_W), lambda i: (l, 0, 0)),
        pl.BlockSpec((None, CONV_W, MIX_W), lambda i: (l, 0, 0)),
        vec,
        pl.BlockSpec((None, POOL_GROUPS, POOL_GD, POOL_GD), lambda i: (l, 0, 0, 0)),
        vec,
        pl.BlockSpec(memory_space=pl.ANY),
    ]
    out_specs = [
        pl.BlockSpec((DEC_SEQ, bs, N_BRANCH * MIX_W), lambda i: (t_block, i, 0)),
        pl.BlockSpec((DEC_SEQ, bs, MIX_W), lambda i: (0, i, 0)),
        pl.BlockSpec((CONV_W - 1, bs, MIX_W), lambda i: (0, i, 0)),
        pl.BlockSpec((POOL_BUF, bs, MIX_W), lambda i: (0, i, 0)),
    ]
    out_shape = [
        jax.ShapeDtypeStruct((n_slabs, SLAB, N_BRANCH * MIX_W), BF),
        jax.ShapeDtypeStruct((DEC_SEQ, DEC_BATCH, MIX_W), F32),
        jax.ShapeDtypeStruct((CONV_W - 1, DEC_BATCH, MIX_W), F32),
        jax.ShapeDtypeStruct((POOL_BUF, DEC_BATCH, MIX_W), F32),
    ]
    return pl.pallas_call(
        functools.partial(_mid_sample_kernel, bs=bs),
        grid=(DEC_BATCH // bs,),
        in_specs=in_specs,
        out_specs=out_specs,
        out_shape=out_shape,
        input_output_aliases={len(in_specs) - 1: 0},
        compiler_params=_params(("arbitrary",), 48),
        name="mid_sample",
    )(proj3, proj3, proj3, proj3, proj3, proj3, state_conv_t, state_pool_t,
      sgu_g.reshape(DEPTH, 1, MIX_W), sgu_b.reshape(DEPTH, 1, MIX_W), wv, bv, conv_w,
      conv_b.reshape(DEPTH, 1, MIX_W), wpool_bf, pool_scale.reshape(DEPTH, 1, MIX_W), br3)


def _merge_kernel(br_ref, w_ref, g0_ref, g1_ref, g2_ref, o_ref, wb_ref):
    @pl.when(pl.program_id(1) == 0)
    def _():
        wb_ref[...] = w_ref[...].astype(BF)

    acc = None
    for r, g_ref in enumerate((g0_ref, g1_ref, g2_ref)):
        term = jax.nn.sigmoid(g_ref[...]) * _dot(br_ref[:, r * MIX_W:(r + 1) * MIX_W], wb_ref[r])
        acc = term if acc is None else acc + term
    o_ref[...] = acc.astype(BF)


def _merge_call(br, proj, w_branch, l):
    tm, tn = 512, 1024
    gate_col0 = 6 * MIX_W // tn
    per_gate = D_MODEL // tn

    def gate(r):
        return pl.BlockSpec((tm, tn), lambda j, i, r=r: (i, gate_col0 + r * per_gate + j))

    return pl.pallas_call(
        _merge_kernel,
        grid=(D_MODEL // tn, M_ROWS // tm),
        in_specs=[
            pl.BlockSpec((tm, N_BRANCH * MIX_W), lambda j, i: (i, 0)),
            pl.BlockSpec((None, N_BRANCH, MIX_W, tn), lambda j, i: (l, 0, 0, j)),
            gate(0), gate(1), gate(2),
        ],
        out_specs=pl.BlockSpec((tm, tn), lambda j, i: (i, j)),
        out_shape=jax.ShapeDtypeStruct((M_ROWS, D_MODEL), BF),
        scratch_shapes=[pltpu.VMEM((N_BRANCH, MIX_W, tn), BF)],
        compiler_params=_params(("arbitrary", "arbitrary"), 56),
        name="merge",
    )(br, w_branch, proj, proj, proj)


def _router_top2(h, wr_ref, te_ref, tw_ref):
    rows = h.shape[0]
    logits = jnp.dot(h, wr_ref[...], preferred_element_type=F32, precision=lax.Precision.HIGHEST)
    lane = lax.broadcasted_iota(jnp.int32, (rows, ROUTER_LANES), 1)
    lane_f = lane.astype(F32)
    valid = lane < N_EXPERTS
    lg = jnp.where(valid, logits, -jnp.inf)
    ex = jnp.exp(lg - jnp.max(lg, axis=-1, keepdims=True))
    probs = ex / jnp.sum(ex, axis=-1, keepdims=True)
    cand = jnp.where(valid, probs, -1.0)
    p1 = jnp.max(cand, axis=-1, keepdims=True)
    i1 = jnp.min(jnp.where(cand == p1, lane_f, float(ROUTER_LANES)), axis=-1, keepdims=True)
    cand = jnp.where(lane_f == i1, -1.0, cand)
    p2 = jnp.max(cand, axis=-1, keepdims=True)
    i2 = jnp.min(jnp.where(cand == p2, lane_f, float(ROUTER_LANES)), axis=-1, keepdims=True)
    tot = p1 + p2
    te_ref[...] = jnp.where(lane == 0, i1, jnp.where(lane == 1, i2, 0.0)).astype(jnp.int32)
    tw_ref[...] = jnp.where(lane == 0, p1 / tot, jnp.where(lane == 1, p2 / tot, 0.0))


def _post_kernel(*refs, rows, nblk, gathered, projected, emit_h, router, h_token_major):
    it = iter(refs)
    if gathered:
        d0c, d1c, d0n, d1n, y_hbm, tw_in_ref = (next(it) for _ in range(6))
    else:
        f_ref = next(it)
    if projected:
        w_ref = next(it)
    x_ref, gate_ref, lng_ref, lnb_ref = (next(it) for _ in range(4))
    sh_ref = sc_ref = wr_ref = ho_ref = te_ref = tw_ref = None
    if emit_h:
        sh_ref, sc_ref = next(it), next(it)
    if router:
        wr_ref = next(it)
    xo_ref = next(it)
    xo_sample_ref = None if emit_h else next(it)
    if emit_h:
        ho_ref = next(it)
    if router:
        te_ref, tw_ref = next(it), next(it)

    i = pl.program_id(0)
    if projected:
        wb_ref = next(it)

        @pl.when(i == 0)
        def _():
            wb_ref[...] = w_ref[...].astype(BF)

    if gathered:
        buf, sem = next(it), next(it)
        slot = i % 2

        def issue(d0_ref, d1_ref, s):
            def body(r, carry):
                pltpu.make_async_copy(y_hbm.at[pl.ds(d0_ref[0, r], 1)], buf.at[s, 0, pl.ds(r, 1)], sem.at[s]).start()
                pltpu.make_async_copy(y_hbm.at[pl.ds(d1_ref[0, r], 1)], buf.at[s, 1, pl.ds(r, 1)], sem.at[s]).start()
                return carry

            lax.fori_loop(0, rows, body, 0)

        @pl.when(i == 0)
        def _():
            issue(d0c, d1c, 0)

        @pl.when(i + 1 < nblk)
        def _():
            issue(d0n, d1n, 1 - slot)

        def wait(r, carry):
            pltpu.make_async_copy(y_hbm.at[pl.ds(0, 1)], buf.at[slot, 0, pl.ds(r, 1)], sem.at[slot]).wait()
            pltpu.make_async_copy(y_hbm.at[pl.ds(0, 1)], buf.at[slot, 1, pl.ds(r, 1)], sem.at[slot]).wait()
            return carry

        lax.fori_loop(0, rows, wait, 0)
        f = buf[slot, 0] * tw_in_ref[:, 0:1] + buf[slot, 1] * tw_in_ref[:, 1:2]
    elif projected:
        f = _dot(f_ref[...], wb_ref[...])
    else:
        f = f_ref[...]

    y = _slab_bcast(lambda a, g: a * g, f, gate_ref[...]) + DEEPNORM_ALPHA * x_ref[...]
    xn = _ln(y, lng_ref[...], lnb_ref[...])
    if emit_h:
        xo_ref[...] = xn
    else:
        @pl.when(i < P_ROWS // rows)
        def _():
            xo_ref[...] = xn

        @pl.when(i >= P_ROWS // rows)
        def _():
            xo_sample_ref[...] = xn

    if emit_h:
        h = _modulate(xn, sh_ref[...], sc_ref[...])
        if h_token_major:
            for s in range(LANE_TILES):
                ho_ref[pl.ds(s, rows, stride=LANE_TILES), :] = h[:, s * 128:(s + 1) * 128]
        else:
            ho_ref[...] = h.astype(ho_ref.dtype)
        if router:
            _router_top2(h, wr_ref, te_ref, tw_ref)


def _post_call(src, x, mods, l, n, ln_g, ln_b, next_mod, *, h_token_major=False, wr=None, wr_layer=0,
               gather=None, proj=None, name):
    rows = POST_ROWS
    nblk = M_ROWS // rows
    emit_h = next_mod is not None
    router = wr is not None
    gathered = gather is not None
    projected = proj is not None
    row = pl.BlockSpec((rows, D_MODEL), lambda i: (i, 0))
    vec = pl.BlockSpec((None, None, 1, D_MODEL), lambda i: (l, n, 0, 0))
    in_specs, args, scratch = [], [], []
    if projected:
        w, w_layer = proj
        in_specs += [row, pl.BlockSpec((None, D_MODEL, D_MODEL), lambda i: (w_layer, 0, 0),
                                       pipeline_mode=pl.Buffered(1))]
        args += [src, w]
        scratch = [pltpu.VMEM((D_MODEL, D_MODEL), BF)]
    if gathered:
        dest0, dest1, tw = gather
        cur = pl.BlockSpec((None, 1, rows), lambda i: (i, 0, 0), memory_space=pltpu.SMEM)
        nxt = pl.BlockSpec((None, 1, rows), lambda i: (jnp.minimum(i + 1, nblk - 1), 0, 0),
                           memory_space=pltpu.SMEM)
        d0, d1 = dest0.reshape(nblk, 1, rows), dest1.reshape(nblk, 1, rows)
        in_specs += [cur, cur, nxt, nxt, pl.BlockSpec(memory_space=pl.ANY),
                     pl.BlockSpec((rows, ROUTER_LANES), lambda i: (i, 0))]
        args += [d0, d1, d0, d1, src, tw]
        scratch = [pltpu.VMEM((2, TOP_K, rows, D_MODEL), F32), pltpu.SemaphoreType.DMA((2,))]
    elif not projected:
        in_specs.append(row)
        args.append(src)
    in_specs += [row, _mod_spec(l, n, 2, rows), vec, vec]
    args += [x, mods, ln_g.reshape(DEPTH, 2, 1, D_MODEL), ln_b.reshape(DEPTH, 2, 1, D_MODEL)]
    if emit_h:
        out_specs = [row]
        out_shape = [jax.ShapeDtypeStruct((M_ROWS, D_MODEL), F32)]
    else:
        prompt_tiles = P_ROWS // rows
        out_specs = [pl.BlockSpec((rows, D_MODEL), lambda i: (jnp.minimum(i, prompt_tiles - 1), 0)),
                     pl.BlockSpec((rows, D_MODEL), lambda i: (jnp.maximum(i - prompt_tiles, 0), 0))]
        out_shape = [jax.ShapeDtypeStruct((P_ROWS, D_MODEL), F32), jax.ShapeDtypeStruct((S_ROWS, D_MODEL), F32)]
    if emit_h:
        nl, nn = next_mod
        in_specs += [_mod_spec(nl, nn, 0, rows), _mod_spec(nl, nn, 1, rows)]
        args += [mods, mods]
        if h_token_major:
            out_specs.append(pl.BlockSpec((rows * LANE_TILES, 128), lambda i: (i, 0)))
            out_shape.append(jax.ShapeDtypeStruct((M_ROWS * LANE_TILES, 128), F32))
        else:
            out_specs.append(row)
            out_shape.append(jax.ShapeDtypeStruct((M_ROWS, D_MODEL), BF))
    if router:
        in_specs.append(pl.BlockSpec((None, D_MODEL, ROUTER_LANES), lambda i: (wr_layer, 0, 0)))
        args.append(wr)
        lane_out = pl.BlockSpec((rows, ROUTER_LANES), lambda i: (i, 0))
        out_specs += [lane_out, lane_out]
        out_shape += [jax.ShapeDtypeStruct((M_ROWS, ROUTER_LANES), jnp.int32),
                      jax.ShapeDtypeStruct((M_ROWS, ROUTER_LANES), F32)]
    return pl.pallas_call(
        functools.partial(_post_kernel, rows=rows, nblk=nblk, gathered=gathered, projected=projected,
                          emit_h=emit_h, router=router, h_token_major=h_token_major),
        grid=(nblk,),
        in_specs=in_specs,
        out_specs=out_specs,
        out_shape=out_shape,
        scratch_shapes=scratch,
        compiler_params=_params(("arbitrary",), 56),
        name=name,
    )(*args)


def _gather_kernel(idx_cur, idx_nxt, live_ref, src_hbm, out_ref, buf, sem, *, rows, nblk):
    i = pl.program_id(0)
    slot = i % 2
    live = live_ref[i] == 1

    def token_copy(tok, s, r):
        dst = pl.multiple_of((s * rows + r) * LANE_TILES, LANE_TILES)
        return pltpu.make_async_copy(src_hbm.at[tok], buf.at[pl.ds(dst, LANE_TILES)], sem.at[s])

    def issue(idx_ref, s):
        def body(r, carry):
            token_copy(idx_ref[0, r], s, r).start()
            return carry

        lax.fori_loop(0, rows, body, 0)

    @pl.when(jnp.logical_and(i == 0, live))
    def _():
        issue(idx_cur, 0)

    @pl.when(jnp.logical_and(i + 1 < nblk, live_ref[jnp.minimum(i + 1, nblk - 1)] == 1))
    def _():
        issue(idx_nxt, 1 - slot)

    @pl.when(live)
    def _():
        def wait(r, carry):
            token_copy(0, slot, r).wait()
            return carry

        lax.fori_loop(0, rows, wait, 0)

    @pl.when(jnp.logical_not(live))
    def _():
        out_ref[...] = jnp.zeros_like(out_ref)

    for static_slot in range(2):
        @pl.when(jnp.logical_and(live, slot == static_slot))
        def _(static_slot=static_slot):
            base = static_slot * rows * LANE_TILES
            for s in range(LANE_TILES):
                out_ref[:, s * 128:(s + 1) * 128] = buf[pl.ds(base + s, rows, stride=LANE_TILES), :].astype(BF)


def _gather_call(src, idx, live):
    rows = GATHER_ROWS
    nblk = idx.shape[0] // rows
    idx3 = idx.reshape(nblk, 1, rows)
    return pl.pallas_call(
        functools.partial(_gather_kernel, rows=rows, nblk=nblk),
        grid=(nblk,),
        in_specs=[
            pl.BlockSpec((None, 1, rows), lambda i: (i, 0, 0), memory_space=pltpu.SMEM),
            pl.BlockSpec((None, 1, rows), lambda i: (jnp.minimum(i + 1, nblk - 1), 0, 0),
                         memory_space=pltpu.SMEM),
            pl.BlockSpec(memory_space=pltpu.SMEM),
            pl.BlockSpec(memory_space=pl.ANY),
        ],
        out_specs=pl.BlockSpec((rows, D_MODEL), lambda i: (i, 0)),
        out_shape=jax.ShapeDtypeStruct((idx.shape[0], D_MODEL), BF),
        scratch_shapes=[pltpu.VMEM((2 * rows * LANE_TILES, 128), F32), pltpu.SemaphoreType.DMA((2,))],
        compiler_params=_params(("arbitrary",), 32),
        name="moe_gather",
    )(idx3, idx3, live, src)


def _dispatch_plan(te):
    flat_e = te[:, :TOP_K].reshape(-1)
    experts = jnp.arange(N_EXPERTS, dtype=jnp.int32)
    onehot = (flat_e[:, None] == experts[None, :]).astype(jnp.int32)
    csum = jnp.cumsum(onehot, axis=0)
    rank = jnp.sum(csum * onehot, axis=1) - 1
    counts = csum[-1]
    padded = (counts + MOE_BLK - 1) // MOE_BLK * MOE_BLK
    pad_end = jnp.cumsum(padded)
    pad_start = pad_end - padded
    dest = (jnp.sum(pad_start[None, :] * onehot, axis=1) + rank).astype(jnp.int32)
    tok_of_row = jnp.zeros((MOE_ROWS,), jnp.int32).at[dest].set(
        jnp.arange(N_ASSIGN, dtype=jnp.int32) // TOP_K)
    dest2 = dest.reshape(M_ROWS, TOP_K)

    valid_end = pad_start + counts

    later_used = jnp.logical_and(experts[None, :] > experts[:, None], (padded > 0)[None, :])
    nxt_of_e = jnp.min(jnp.where(later_used, experts[None, :], N_EXPERTS), axis=1)
    nxt_of_e = jnp.where(nxt_of_e == N_EXPERTS, -1, nxt_of_e)

    def plan(blk):
        ids = jnp.arange(MOE_ROWS // blk, dtype=jnp.int32)
        starts = ids * blk
        in_range = jnp.minimum(starts, pad_end[-1] - blk)
        block_e = jnp.sum((pad_end[None, :] <= in_range[:, None]).astype(jnp.int32), axis=1)
        onehot_e = (block_e[:, None] == experts[None, :]).astype(jnp.int32)
        valid = jnp.clip(jnp.sum(valid_end[None, :] * onehot_e, axis=1) - starts, 0, blk)
        live = (valid > 0).astype(jnp.int32)
        src = lax.cummax(jnp.where(live == 1, ids, 0), axis=0)
        nxt = jnp.sum(nxt_of_e[None, :] * onehot_e, axis=1)
        return block_e.astype(jnp.int32), src.astype(jnp.int32), live, nxt.astype(jnp.int32)

    return tok_of_row, dest2[:, 0], dest2[:, 1], plan


def kernel(x_prompt, x_sample, state_conv, state_pool, c_prompt, c_sample, ln_in_g, ln_in_b, w_ada, b_ada,
           w_mix_in, sgu_g, sgu_b, w_spatial, b_spatial, conv_w, conv_b, w_pool, pool_scale, w_branch, w_o,
           ln_g, ln_b, w_ffn_gate, w_ffn_up, w_ffn_down, w_router, w_exp_gate, w_exp_up, w_exp_down):
    c_all = jnp.concatenate(
        [c_sample, c_prompt, jnp.zeros((C_ROWS - BATCH - DEC_BATCH, D_MODEL), F32)], axis=0)
    mods = _ada_call(c_all, w_ada, b_ada)

    wsp_bf = w_spatial.astype(BF)
    bsp_full = jnp.repeat(b_spatial.transpose(0, 2, 1), SGU_HD, axis=-1)
    wv = jnp.repeat(w_spatial[:, :, :DEC_SEQ, :DEC_SEQ].transpose(0, 2, 3, 1), SGU_HD, axis=-1)
    bv = bsp_full[:, :DEC_SEQ]
    wpool_bf = w_pool.astype(BF)
    state_conv_t = state_conv.transpose(0, 2, 1, 3)
    state_pool_t = state_pool.transpose(0, 2, 1, 3)
    wr_pad = jnp.pad(w_router, ((0, 0), (0, 0), (0, ROUTER_LANES - N_EXPERTS)))

    x, h = _ln_in_call(x_prompt.reshape(P_ROWS, D_MODEL), x_sample.transpose(1, 0, 2).reshape(S_ROWS, D_MODEL),
                       ln_in_g, ln_in_b, mods)

    conv_p, conv_s, pool_p, pool_s, v_s = [], [], [], [], []
    for l in range(DEPTH):
        i = l // 2
        proj = _dense_mm(h, [w_mix_in], l, bm=1536, tn=1024, vmem_mib=48, name="mix_in")
        br, cp, pp = _mid_prompt_call(proj, l, sgu_g, sgu_b, wsp_bf, bsp_full, conv_w, conv_b, wpool_bf,
                                      pool_scale)
        br3, vs, cs, ps = _mid_sample_call(
            proj.reshape(M_ROWS // SLAB, SLAB, MIX_IN_COLS), br.reshape(M_ROWS // SLAB, SLAB, N_BRANCH * MIX_W),
            l, state_conv_t, state_pool_t, sgu_g, sgu_b, wv, bv, conv_w, conv_b, wpool_bf, pool_scale)
        conv_p.append(cp[:, 8 - (CONV_W - 1):, :])
        pool_p.append(pp[:, 16 - POOL_BUF:, :])
        conv_s.append(cs.transpose(1, 0, 2))
        pool_s.append(ps.transpose(1, 0, 2))
        v_s.append(vs.transpose(1, 0, 2))
        merged = _merge_call(br3.reshape(M_ROWS, N_BRANCH * MIX_W), proj, w_branch, l)
        next_mod = (l + 1, 0) if l + 1 < DEPTH else None
        if l % 2 == 0:
            x, h2 = _post_call(merged, x, mods, l, 0, ln_g, ln_b, (l, 1), proj=(w_o, l), name="out_proj_post")
            hid = _dense_mm(h2, [w_ffn_gate, w_ffn_up], i, bm=1024, tn=512, swiglu=True, out_dtype=BF,
                            vmem_mib=48, name="ffn_up")
            f = _dense_mm(hid, [w_ffn_down], i, bm=1024, tn=512, vmem_mib=56, name="ffn_down")
            outs = _post_call(f, x, mods, l, 1, ln_g, ln_b, next_mod, name="post_ffn")
        else:
            mix = _dense_mm(merged, [w_o], l, bm=1536, tn=1024, vmem_mib=48, name="out_proj")
            x, h2, te, tw = _post_call(mix, x, mods, l, 0, ln_g, ln_b, (l, 1), h_token_major=True, wr=wr_pad,
                                       wr_layer=i, name="post_mix_router")
            tok_of_row, dest0, dest1, plan = _dispatch_plan(te)
            rows = _gather_call(h2.reshape(M_ROWS, LANE_TILES, 128), tok_of_row, plan(GATHER_ROWS)[2])
            moe_plan = plan(MOE_BLK)
            hid = _gmm_call(rows, [w_exp_gate, w_exp_up], i, moe_plan, bm=MOE_BLK, tn=1024, swiglu=True,
                            out_dtype=BF, vmem_mib=48, name="moe_up")
            y = _gmm_call(hid, [w_exp_down], i, moe_plan, bm=MOE_BLK, tn=512, swiglu=False,
                          out_dtype=F32, vmem_mib=56, name="moe_down")
            outs = _post_call(y, x, mods, l, 1, ln_g, ln_b, next_mod, gather=(dest0, dest1, tw),
                              name="moe_combine")
        if next_mod is None:
            x_prompt_out, x_sample_out = outs
        else:
            x, h = outs

    y_prompt = x_prompt_out.reshape(BATCH, SEQ, D_MODEL)
    y_sample = x_sample_out.reshape(DEC_SEQ, DEC_BATCH, D_MODEL).transpose(1, 0, 2)
    return (y_prompt, y_sample, jnp.stack(conv_p, 0), jnp.stack(conv_s, 0), jnp.stack(pool_p, 0),
            jnp.stack(pool_s, 0), jnp.stack(v_s, 0))
```
